```python
import jax, jax.numpy as jnp
from jax import lax
import numpy as np

D_MODEL = 2048
BATCH = 2
SEQ = 16384
DEPTH = 1
DEC_BATCH = 16
DEC_SEQ = 64
PAST_LEN = 1024

CHUNK = 64
D_CONV = D_MODEL
CONV_WIDTH = 31
N_RET_HEADS = D_MODEL // 256
RET_DK = 256
RET_DV = 512
D_RET_QK = N_RET_HEADS * RET_DK
D_RET_V = N_RET_HEADS * RET_DV
D_FF = ((8 * D_MODEL + 3 * 256 - 1) // (3 * 256)) * 256
ROPE_BASE = 10000.0
EPS = 1e-6

SECTION_WIDTHS = (D_CONV, D_CONV, D_RET_QK, D_RET_QK, D_RET_V, D_RET_V, D_MODEL, D_MODEL)
SECTION_OFFSETS = tuple(int(o) for o in np.concatenate([[0], np.cumsum(SECTION_WIDTHS)]))
IN_COLS = SECTION_OFFSETS[-1]

kernel_name = "hybrid_conformer_conv_retention_stream_step"


def rms_norm(x, w):
    x32 = x.astype(jnp.float32)
    y = x32 * lax.rsqrt(jnp.mean(x32 * x32, axis=-1, keepdims=True) + EPS) * w.astype(jnp.float32)
    return y.astype(x.dtype)


def layer_norm(x, g, b):
    x32 = x.astype(jnp.float32)
    mu = jnp.mean(x32, axis=-1, keepdims=True)
    var = jnp.mean(jnp.square(x32 - mu), axis=-1, keepdims=True)
    y = (x32 - mu) * lax.rsqrt(var + EPS) * g.astype(jnp.float32) + b.astype(jnp.float32)
    return y.astype(x.dtype)


def rotary(x, pos):
    half = x.shape[-1] // 2
    inv_freq = 1.0 / (ROPE_BASE ** (jnp.arange(half, dtype=jnp.float32) / half))
    ang = pos[:, None] * inv_freq[None, :]
    cos = jnp.cos(ang)[None, :, None, :]
    sin = jnp.sin(ang)[None, :, None, :]
    x1, x2 = x[..., :half], x[..., half:]
    return jnp.concatenate([x1 * cos - x2 * sin, x1 * sin + x2 * cos], axis=-1)


def retention_log_decay():
    return jnp.log1p(-jnp.exp2(-5.0 - jnp.arange(N_RET_HEADS, dtype=jnp.float32)))


def retention_chunk(S, qkv, log_g):
    q, k, v = qkv
    L = q.shape[2]
    idx = jnp.arange(L, dtype=jnp.float32)
    diff = idx[:, None] - idx[None, :]
    decay = jnp.where(diff[None] >= 0.0,
                      jnp.exp(jnp.maximum(diff, 0.0)[None] * log_g[:, None, None]), 0.0)
    scores = jnp.einsum('bhnd,bhmd->bhnm', q, k) * decay[None]
    o = jnp.einsum('bhnm,bhme->bhne', scores, v)
    xi = jnp.exp((idx + 1.0)[None, :] * log_g[:, None])
    o = o + jnp.einsum('bhnd,bhde->bhne', q, S) * xi[None, :, :, None]
    zeta = jnp.exp((L - 1.0 - idx)[None, :] * log_g[:, None])
    S_new = (jnp.exp(L * log_g)[None, :, None, None] * S
             + jnp.einsum('bhmd,bhme->bhde', k * zeta[None, :, :, None], v))
    return S_new, o


def retention(q, k, v, S0):
    Bt, T = q.shape[0], q.shape[1]
    L = CHUNK if T % CHUNK == 0 else T
    n = T // L
    log_g = retention_log_decay()

    def to_chunks(t):
        return t.reshape(Bt, n, L, N_RET_HEADS, t.shape[-1]).transpose(1, 0, 3, 2, 4)

    S, o = lax.scan(lambda s, c: retention_chunk(s, c, log_g), S0,
                    (to_chunks(q), to_chunks(k), to_chunks(v)))
    o = o.transpose(1, 0, 3, 2, 4).reshape(Bt, T, N_RET_HEADS, RET_DV)
    return o, S


def trunk_layer(x, conv_hist, ret_state, start,
                norm_mix_pre, norm_mix_post, w_in, b_gate, conv_dw, conv_dw_b,
                conv_ln_g, conv_ln_b, w_conv_out, ret_gn_g, w_ret_out, w_out,
                norm_ffn_pre, norm_ffn_post, w_ffn_gate, w_ffn_up, w_ffn_down):
    Bt, T, _ = x.shape
    h = rms_norm(x, norm_mix_pre)

    def section(i):
        return jnp.einsum('btd,dc->btc', h, w_in[:, SECTION_OFFSETS[i]:SECTION_OFFSETS[i + 1]])

    a = section(0) * jax.nn.sigmoid(section(1))
    a_cat = jnp.concatenate([conv_hist.astype(a.dtype), a], axis=1)
    new_hist = a_cat[:, -(CONV_WIDTH - 1):]
    c = lax.conv_general_dilated(a_cat, conv_dw[:, None, :].astype(a.dtype), (1,), 'VALID',
                                 dimension_numbers=('NWC', 'WIO', 'NWC'),
                                 feature_group_count=D_CONV) + conv_dw_b
    c = jax.nn.silu(layer_norm(c, conv_ln_g, conv_ln_b))
    y_conv = jnp.einsum('btc,cd->btd', c, w_conv_out)

    pos = start + jnp.arange(T, dtype=jnp.float32)
    q = rotary(section(2).reshape(Bt, T, N_RET_HEADS, RET_DK).astype(jnp.float32), pos)
    k = rotary(section(3).reshape(Bt, T, N_RET_HEADS, RET_DK).astype(jnp.float32), pos) * (RET_DK ** -0.5)
    v = section(4).reshape(Bt, T, N_RET_HEADS, RET_DV).astype(jnp.float32)
    o, S_new = retention(q, k, v, ret_state.astype(jnp.float32))
    mu = jnp.mean(o, axis=-1, keepdims=True)
    var = jnp.mean(jnp.square(o - mu), axis=-1, keepdims=True)
    o = ((o - mu) * lax.rsqrt(var + EPS)).reshape(Bt, T, D_RET_V) * ret_gn_g.astype(jnp.float32)
    o = o.astype(x.dtype) * jax.nn.silu(section(5))
    y_ret = jnp.einsum('bte,ed->btd', o, w_ret_out)

    gate_c = jax.nn.sigmoid(section(6) + b_gate[:D_MODEL])
    gate_r = jax.nn.sigmoid(section(7) + b_gate[D_MODEL:])
    mix = jnp.einsum('btd,de->bte', gate_c * y_conv + gate_r * y_ret, w_out)
    x = x + rms_norm(mix, norm_mix_post)

    h2 = rms_norm(x, norm_ffn_pre)
    f = jax.nn.silu(jnp.einsum('btd,df->btf', h2, w_ffn_gate)) * jnp.einsum('btd,df->btf', h2, w_ffn_up)
    f = jnp.einsum('btf,fd->btd', f, w_ffn_down)
    x = x + rms_norm(f, norm_ffn_post)
    return x, new_hist, S_new.astype(ret_state.dtype)


def setup_inputs(seed: int = 0) -> dict:
    key = jax.random.key(seed)
    ks = jax.random.split(key, 24)

    def nrm(k, shape, scale):
        return jax.random.normal(k, shape, dtype=jnp.float32) * scale

    def gain(k, shape):
        return 1.0 + nrm(k, shape, 0.05)

    return {
        "x_prompt": nrm(ks[0], (BATCH, SEQ, D_MODEL), 1.0),
        "x_sample": nrm(ks[1], (DEC_BATCH, DEC_SEQ, D_MODEL), 1.0),
        "cache_conv": nrm(ks[2], (DEPTH, DEC_BATCH, CONV_WIDTH - 1, D_CONV), 0.5),
        "state_ret": nrm(ks[3], (DEPTH, DEC_BATCH, N_RET_HEADS, RET_DK, RET_DV), 0.1),
        "norm_mix_pre": gain(ks[4], (DEPTH, D_MODEL)),
        "norm_mix_post": gain(ks[5], (DEPTH, D_MODEL)),
        "w_in": nrm(ks[6], (DEPTH, D_MODEL, IN_COLS), D_MODEL ** -0.5),
        "b_gate": nrm(ks[7], (DEPTH, 2 * D_MODEL), 0.02),
        "conv_dw": nrm(ks[8], (DEPTH, CONV_WIDTH, D_CONV), CONV_WIDTH ** -0.5),
        "conv_dw_b": nrm(ks[9], (DEPTH, D_CONV), 0.02),
        "conv_ln_g": gain(ks[10], (DEPTH, D_CONV)),
        "conv_ln_b": nrm(ks[11], (DEPTH, D_CONV), 0.02),
        "w_conv_out": nrm(ks[12], (DEPTH, D_CONV, D_MODEL), D_CONV ** -0.5),
        "ret_gn_g": gain(ks[13], (DEPTH, D_RET_V)),
        "w_ret_out": nrm(ks[14], (DEPTH, D_RET_V, D_MODEL), D_RET_V ** -0.5),
        "w_out": nrm(ks[15], (DEPTH, D_MODEL, D_MODEL), D_MODEL ** -0.5),
        "norm_ffn_pre": gain(ks[16], (DEPTH, D_MODEL)),
        "norm_ffn_post": gain(ks[17], (DEPTH, D_MODEL)),
        "w_ffn_gate": nrm(ks[18], (DEPTH, D_MODEL, D_FF), D_MODEL ** -0.5),
        "w_ffn_up": nrm(ks[19], (DEPTH, D_MODEL, D_FF), D_MODEL ** -0.5),
        "w_ffn_down": nrm(ks[20], (DEPTH, D_FF, D_MODEL), D_FF ** -0.5),
    }


def reference(x_prompt, x_sample, cache_conv, state_ret, norm_mix_pre, norm_mix_post, w_in, b_gate,
              conv_dw, conv_dw_b, conv_ln_g, conv_ln_b, w_conv_out, ret_gn_g, w_ret_out, w_out,
              norm_ffn_pre, norm_ffn_post, w_ffn_gate, w_ffn_up, w_ffn_down):
    y_prompt, y_sample = x_prompt, x_sample
    conv_p, ret_p, conv_s, ret_s = [], [], [], []
    for l in range(DEPTH):
        params = (norm_mix_pre[l], norm_mix_post[l], w_in[l], b_gate[l], conv_dw[l], conv_dw_b[l],
                  conv_ln_g[l], conv_ln_b[l], w_conv_out[l], ret_gn_g[l], w_ret_out[l], w_out[l],
                  norm_ffn_pre[l], norm_ffn_post[l], w_ffn_gate[l], w_ffn_up[l], w_ffn_down[l])
        zero_hist = jnp.zeros((BATCH, CONV_WIDTH - 1, D_CONV), x_prompt.dtype)
        zero_state = jnp.zeros((BATCH, N_RET_HEADS, RET_DK, RET_DV), x_prompt.dtype)
        y_prompt, hp, sp = trunk_layer(y_prompt, zero_hist, zero_state, 0, *params)
        y_sample, hs, ss = trunk_layer(y_sample, cache_conv[l], state_ret[l], PAST_LEN, *params)
        conv_p.append(hp)
        ret_p.append(sp)
        conv_s.append(hs)
        ret_s.append(ss)
    return (y_prompt, y_sample, jnp.stack(conv_p), jnp.stack(ret_p), jnp.stack(conv_s), jnp.stack(ret_s))
```

```python
import functools
import math

import jax
import jax.numpy as jnp
from jax import lax
from jax.experimental import pallas as pl
from jax.experimental.pallas import tpu as pltpu

F32 = jnp.float32
BF16 = jnp.bfloat16

EPS = 1e-6
ROPE_BASE = 10000.0
CONV_WIDTH = 31
N_RET_HEADS = 8
RET_DK = 256
RET_DV = 512
ROPE_HALF = RET_DK // 2

MIB = 1024 * 1024
V7X_VMEM_REQUEST_CAP = 56 * MIB
CONV_HALO_ROWS = 32


def _params(est_bytes):
    limit = min(V7X_VMEM_REQUEST_CAP, max(32 * MIB, int(est_bytes * 1.25)))
    return pltpu.CompilerParams(vmem_limit_bytes=limit)


def _nbytes(shape, dtype):
    return math.prod(shape) * jnp.dtype(dtype).itemsize


def _sigmoid(x):
    return jax.nn.sigmoid(x)


def _silu(x):
    return x * jax.nn.sigmoid(x)


def _rms_kernel(x_ref, w_ref, o_ref):
    x = x_ref[...]
    ms = jnp.mean(x * x, axis=-1, keepdims=True)
    o_ref[...] = (x * lax.rsqrt(ms + EPS) * w_ref[...]).astype(o_ref.dtype)


def _rms_norm_bf16(x, w, tm):
    rows, d = x.shape
    est = 2 * tm * d * 4 + 2 * tm * d * 2
    return pl.pallas_call(
        _rms_kernel,
        grid=(rows // tm,),
        in_specs=[pl.BlockSpec((tm, d), lambda i: (i, 0)),
                  pl.BlockSpec((1, d), lambda i: (0, 0))],
        out_specs=pl.BlockSpec((tm, d), lambda i: (i, 0)),
        out_shape=jax.ShapeDtypeStruct((rows, d), BF16),
        compiler_params=_params(est),
        name="rms_norm",
    )(x, w.reshape(1, d))


def _mm_kernel(*refs, n_w, epilogue):
    a_ref = refs[0]
    w_refs = refs[1:1 + n_w]
    extra_refs = refs[1 + n_w:-1]
    o_ref = refs[-1]
    a = a_ref[...]
    accs = [jnp.dot(a, w_ref[...], preferred_element_type=F32) for w_ref in w_refs]
    o_ref[...] = epilogue(accs, extra_refs).astype(o_ref.dtype)


def _matmul(a, ws, extras, epilogue, n_out, out_dtype, tm, tn, name):
    rows, k = a.shape
    grid = (rows // tm, n_out // tn)
    in_specs = [pl.BlockSpec((tm, k), lambda i, j: (i, 0))]
    args = [a]
    est = 2 * tm * k * 2
    for w, off in ws:
        assert off % tn == 0
        in_specs.append(pl.BlockSpec((k, tn), functools.partial(lambda i, j, o: (0, o + j), o=off // tn)))
        args.append(w)
        est += 2 * k * tn * 2
    for arr, spec in extras:
        in_specs.append(spec)
        args.append(arr)
        est += 2 * _nbytes(spec.block_shape, arr.dtype)
    est += 2 * tm * tn * jnp.dtype(out_dtype).itemsize
    est += (len(ws) + 1) * tm * tn * 4
    return pl.pallas_call(
        functools.partial(_mm_kernel, n_w=len(ws), epilogue=epilogue),
        grid=grid,
        in_specs=in_specs,
        out_specs=pl.BlockSpec((tm, tn), lambda i, j: (i, j)),
        out_shape=jax.ShapeDtypeStruct((rows, n_out), out_dtype),
        compiler_params=_params(est),
        name=name,
    )(*args)


def _tile_spec(tm, tn, off):
    return pl.BlockSpec((tm, tn), functools.partial(lambda i, j, o: (i, o + j), o=off // tn))


def _row_spec(tn, off):
    return pl.BlockSpec((1, tn), functools.partial(lambda i, j, o: (0, o + j), o=off // tn))


def _epi_glu(accs, extras):
    return accs[0] * _sigmoid(accs[1])


def _epi_identity(accs, extras):
    return accs[0]


def _epi_silu(accs, extras):
    return _silu(accs[0])


def _epi_sigmoid_bias(accs, extras):
    return _sigmoid(accs[0] + extras[0][...])


def _epi_rotary(accs, extras, *, tn, n_q_tiles):
    cos = extras[0][...]
    sin = extras[1][...]
    scale = jnp.where(pl.program_id(1) >= n_q_tiles, RET_DK ** -0.5, 1.0).astype(F32)
    acc = accs[0]
    outs = []
    for h in range(tn // RET_DK):
        x1 = acc[:, h * RET_DK:h * RET_DK + ROPE_HALF]
        x2 = acc[:, h * RET_DK + ROPE_HALF:(h + 1) * RET_DK]
        outs.append((x1 * cos - x2 * sin) * scale)
        outs.append((x1 * sin + x2 * cos) * scale)
    return jnp.concatenate(outs, axis=-1)


def _epi_gate_mul(accs, extras):
    return extras[0][...].astype(F32) * accs[0]


def _epi_gate_mul_add(accs, extras):
    return extras[0][...].astype(F32) * accs[0] + extras[1][...]


def _epi_swiglu(accs, extras):
    return _silu(accs[0]) * accs[1]


def _rope_kernel(inv_ref, cos_ref, sin_ref, *, start, seq_len, tt):
    row = pl.program_id(0) * tt + lax.broadcasted_iota(jnp.int32, (tt, ROPE_HALF), 0)
    pos = (start + lax.rem(row, seq_len)).astype(F32)
    ang = pos * inv_ref[...]
    cos_ref[...] = jnp.cos(ang)
    sin_ref[...] = jnp.sin(ang)


def _rope_tables(start, seq_len, n_rows, tt):
    inv_freq = 1.0 / (ROPE_BASE ** (jnp.arange(ROPE_HALF, dtype=F32) / ROPE_HALF))
    shape = jax.ShapeDtypeStruct((n_rows, ROPE_HALF), F32)
    return pl.pallas_call(
        functools.partial(_rope_kernel, start=start, seq_len=seq_len, tt=tt),
        grid=(n_rows // tt,),
        in_specs=[pl.BlockSpec((1, ROPE_HALF), lambda i: (0, 0))],
        out_specs=[pl.BlockSpec((tt, ROPE_HALF), lambda i: (i, 0))] * 2,
        out_shape=[shape, shape],
        name="rope_tables",
    )(inv_freq.reshape(1, ROPE_HALF))


def _conv_kernel(cur_ref, prev_ref, hist_ref, w_ref, b_ref, g_ref, beta_ref, o_ref,
                 win_ref, pre_ref, *, tt, row_blk, col_blk):
    n_hist = CONV_WIDTH - 1
    lead = CONV_HALO_ROWS - n_hist
    t = pl.program_id(1)

    @pl.when(t == 0)
    def _():
        win_ref[lead:CONV_HALO_ROWS, :] = hist_ref[0]

    @pl.when(t > 0)
    def _():
        win_ref[0:CONV_HALO_ROWS, :] = prev_ref[0]

    win_ref[CONV_HALO_ROWS:CONV_HALO_ROWS + tt, :] = cur_ref[0]

    d = cur_ref.shape[-1]
    for cb in range(d // col_blk):
        cs = slice(cb * col_blk, (cb + 1) * col_blk)
        for rb in range(tt // row_blk):
            acc = jnp.broadcast_to(b_ref[:, cs], (row_blk, col_blk))
            for j in range(CONV_WIDTH):
                r0 = lead + j + rb * row_blk
                acc = acc + w_ref[j:j + 1, cs] * win_ref[r0:r0 + row_blk, cs]
            pre_ref[rb * row_blk:(rb + 1) * row_blk, cs] = acc

    c = pre_ref[...]
    mu = jnp.mean(c, axis=-1, keepdims=True)
    cc = c - mu
    var = jnp.mean(cc * cc, axis=-1, keepdims=True)
    y = cc * lax.rsqrt(var + EPS) * g_ref[...] + beta_ref[...]
    o_ref[0] = _silu(y).astype(o_ref.dtype)


def _conv_module(a, hist, conv_dw, conv_dw_b, ln_g, ln_b, tt):
    bsz, seq, d = a.shape
    n_hist = CONV_WIDTH - 1
    halo_per_tile = tt // CONV_HALO_ROWS
    row_blk = min(tt, 64)
    col_blk = 256
    est = (2 * tt * d * 4 + 2 * CONV_HALO_ROWS * d * 4 + 2 * CONV_HALO_ROWS * d * 4
           + 2 * tt * d * 2 + (tt + CONV_HALO_ROWS) * d * 4 + 3 * tt * d * 4)
    vec = lambda v: v.reshape(1, d)
    const2 = lambda b, t: (0, 0)
    return pl.pallas_call(
        functools.partial(_conv_kernel, tt=tt, row_blk=row_blk, col_blk=col_blk),
        grid=(bsz, seq // tt),
        in_specs=[
            pl.BlockSpec((1, tt, d), lambda b, t: (b, t, 0)),
            pl.BlockSpec((1, CONV_HALO_ROWS, d),
                         lambda b, t: (b, jnp.maximum(t * halo_per_tile - 1, 0), 0)),
            pl.BlockSpec((1, n_hist, d), lambda b, t: (b, 0, 0)),
            pl.BlockSpec((CONV_WIDTH, d), const2),
            pl.BlockSpec((1, d), const2),
            pl.BlockSpec((1, d), const2),
            pl.BlockSpec((1, d), const2),
        ],
        out_specs=pl.BlockSpec((1, tt, d), lambda b, t: (b, t, 0)),
        out_shape=jax.ShapeDtypeStruct((bsz, seq, d), BF16),
        scratch_shapes=[pltpu.VMEM((CONV_HALO_ROWS + tt, d), F32),
                        pltpu.VMEM((tt, d), F32)],
        compiler_params=_params(est),
        name="conv_module",
    )(a, a, hist, conv_dw, vec(conv_dw_b), vec(ln_g), vec(ln_b))


def _log_decay(h):
    return math.log1p(-(2.0 ** (-5.0 - h)))


def _retention_kernel(q_ref, k_ref, v_ref, sg_ref, s0_ref, gn_ref, o_ref, s_out_ref,
                      s_ref, decay_ref, *, chunk, n_chunks):
    b = pl.program_id(0)
    c = pl.program_id(1)

    @pl.when((b == 0) & (c == 0))
    def _():
        n = lax.broadcasted_iota(jnp.int32, (chunk, chunk), 0)
        m = lax.broadcasted_iota(jnp.int32, (chunk, chunk), 1)
        diff = (n - m).astype(F32)
        for h in range(N_RET_HEADS):
            decay_ref[h] = jnp.where(diff >= 0.0, jnp.exp(jnp.maximum(diff, 0.0) * _log_decay(h)), 0.0)

    @pl.when(c == 0)
    def _():
        s_ref[...] = s0_ref[0]

    idx = lax.broadcasted_iota(jnp.int32, (chunk, 1), 0).astype(F32)
    for h in range(N_RET_HEADS):
        lg = _log_decay(h)
        q = q_ref[:, h * RET_DK:(h + 1) * RET_DK]
        k = k_ref[:, h * RET_DK:(h + 1) * RET_DK]
        v = v_ref[:, h * RET_DV:(h + 1) * RET_DV]
        s_old = s_ref[h]
        scores = lax.dot_general(q, k, (((1,), (1,)), ((), ())), preferred_element_type=F32)
        scores = (scores * decay_ref[h]).astype(BF16)
        o = jnp.dot(scores, v, preferred_element_type=F32)
        xi = jnp.exp((idx + 1.0) * lg)
        o = o + jnp.dot(q, s_old.astype(BF16), preferred_element_type=F32) * xi
        zeta = jnp.exp((chunk - 1.0 - idx) * lg)
        kz = (k.astype(F32) * zeta).astype(BF16)
        s_ref[h] = math.exp(chunk * lg) * s_old + lax.dot_general(
            kz, v, (((0,), (0,)), ((), ())), preferred_element_type=F32)
        mu = jnp.mean(o, axis=-1, keepdims=True)
        oc = o - mu
        var = jnp.mean(oc * oc, axis=-1, keepdims=True)
        on = oc * lax.rsqrt(var + EPS) * gn_ref[:, h * RET_DV:(h + 1) * RET_DV]
        gate = sg_ref[:, h * RET_DV:(h + 1) * RET_DV].astype(F32)
        o_ref[:, h * RET_DV:(h + 1) * RET_DV] = (on * gate).astype(o_ref.dtype)

    @pl.when(c == n_chunks - 1)
    def _():
        s_out_ref[0] = s_ref[...]


def _retention(qk, v, sg, state, gn_g, bsz, seq, chunk):
    rows = bsz * seq
    n_chunks = seq // chunk
    dqk = N_RET_HEADS * RET_DK
    dv = N_RET_HEADS * RET_DV
    state_blk = (1, N_RET_HEADS, RET_DK, RET_DV)
    est = (2 * 2 * chunk * dqk * 2 + 3 * 2 * chunk * dv * 2 + 4 * _nbytes(state_blk, F32)
           + _nbytes(state_blk, F32) + N_RET_HEADS * chunk * chunk * 4 + 8 * chunk * RET_DV * 4)
    row_blk = lambda b, c: (b * n_chunks + c, 0)
    return pl.pallas_call(
        functools.partial(_retention_kernel, chunk=chunk, n_chunks=n_chunks),
        grid=(bsz, n_chunks),
        in_specs=[
            pl.BlockSpec((chunk, dqk), row_blk),
            pl.BlockSpec((chunk, dqk), lambda b, c: (b * n_chunks + c, 1)),
            pl.BlockSpec((chunk, dv), row_blk),
            pl.BlockSpec((chunk, dv), row_blk),
            pl.BlockSpec(state_blk, lambda b, c: (b, 0, 0, 0)),
            pl.BlockSpec((1, dv), lambda b, c: (0, 0)),
        ],
        out_specs=[pl.BlockSpec((chunk, dv), row_blk),
                   pl.BlockSpec(state_blk, lambda b, c: (b, 0, 0, 0))],
        out_shape=[jax.ShapeDtypeStruct((rows, dv), BF16),
                   jax.ShapeDtypeStruct((bsz,) + state_blk[1:], F32)],
        scratch_shapes=[pltpu.VMEM(state_blk[1:], F32),
                        pltpu.VMEM((N_RET_HEADS, chunk, chunk), F32)],
        compiler_params=_params(est),
        name="retention",
    )(qk, qk, v, sg, state, gn_g.reshape(1, dv))


def _mm_norm_kernel(*refs, n_j, emit_next):
    if emit_next:
        a_ref, w_ref, res_ref, nw_ref, nw2_ref, o_ref, h_ref, acc_ref = refs
    else:
        a_ref, w_ref, res_ref, nw_ref, o_ref, acc_ref = refs
    j = pl.program_id(1)
    acc_ref[j] = jnp.dot(a_ref[...], w_ref[...], preferred_element_type=F32)

    @pl.when(j == n_j - 1)
    def _():
        tn = acc_ref.shape[-1]
        d = n_j * tn
        ss = None
        for jj in range(n_j):
            m = acc_ref[jj]
            part = jnp.sum(m * m, axis=-1, keepdims=True)
            ss = part if ss is None else ss + part
        inv = lax.rsqrt(ss / d + EPS)
        ss2 = None
        for jj in range(n_j):
            cs = slice(jj * tn, (jj + 1) * tn)
            y = res_ref[:, cs] + acc_ref[jj] * inv * nw_ref[:, cs]
            o_ref[:, cs] = y
            if emit_next:
                part = jnp.sum(y * y, axis=-1, keepdims=True)
                ss2 = part if ss2 is None else ss2 + part
        if emit_next:
            inv2 = lax.rsqrt(ss2 / d + EPS)
            for jj in range(n_j):
                cs = slice(jj * tn, (jj + 1) * tn)
                h_ref[:, cs] = (o_ref[:, cs] * inv2 * nw2_ref[:, cs]).astype(h_ref.dtype)


def _matmul_norm_residual(a, w, res, norm_w, next_norm_w, tm, tn, name):
    rows, k = a.shape
    d = w.shape[1]
    n_j = d // tn
    emit_next = next_norm_w is not None
    row_i = lambda i, j: (i, 0)
    const2 = lambda i, j: (0, 0)
    in_specs = [pl.BlockSpec((tm, k), row_i),
                pl.BlockSpec((k, tn), lambda i, j: (0, j)),
                pl.BlockSpec((tm, d), row_i),
                pl.BlockSpec((1, d), const2)]
    args = [a, w, res, norm_w.reshape(1, d)]
    out_specs = [pl.BlockSpec((tm, d), row_i)]
    out_shape = [jax.ShapeDtypeStruct((rows, d), F32)]
    est = 2 * tm * k * 2 + 2 * k * tn * 2 + 2 * tm * d * 4 + 2 * tm * d * 4 + tm * d * 4 + 2 * tm * tn * 4
    if emit_next:
        in_specs.append(pl.BlockSpec((1, d), const2))
        args.append(next_norm_w.reshape(1, d))
        out_specs.append(pl.BlockSpec((tm, d), row_i))
        out_shape.append(jax.ShapeDtypeStruct((rows, d), BF16))
        est += 2 * tm * d * 2
    outs = pl.pallas_call(
        functools.partial(_mm_norm_kernel, n_j=n_j, emit_next=emit_next),
        grid=(rows // tm, n_j),
        in_specs=in_specs,
        out_specs=out_specs,
        out_shape=out_shape,
        scratch_shapes=[pltpu.VMEM((n_j, tm, tn), F32)],
        compiler_params=_params(est),
        name=name,
    )(*args)
    return outs if emit_next else outs[0]


def _trunk_layer(x, conv_hist, ret_state, start, p):
    bsz, seq, d = x.shape
    rows = bsz * seq
    x2 = x.reshape(rows, d)
    d_qk = N_RET_HEADS * RET_DK
    d_v = N_RET_HEADS * RET_DV
    off_glu_v, off_glu_g = 0, d
    off_qk = 2 * d
    off_v = off_qk + 2 * d_qk
    off_sg = off_v + d_v
    off_gate = off_sg + d_v

    tm = min(rows, 1024)
    tn = 1024
    w_in = p["w_in"]

    h = _rms_norm_bf16(x2, p["norm_mix_pre"], min(rows, 512))

    a = _matmul(h, [(w_in, off_glu_v), (w_in, off_glu_g)], [], _epi_glu, d, F32, tm, 512, "in_proj_glu")
    a3 = a.reshape(bsz, seq, d)
    new_hist = a3[:, seq - (CONV_WIDTH - 1):, :]
    conv_tt = min(seq, 128)
    c = _conv_module(a3, conv_hist, p["conv_dw"], p["conv_dw_b"], p["conv_ln_g"], p["conv_ln_b"], conv_tt)
    c = c.reshape(rows, d)

    n_tab = max(seq, tm)
    cos_t, sin_t = _rope_tables(start, seq, n_tab, min(n_tab, 1024))
    tab_tiles = n_tab // tm
    tab_spec = pl.BlockSpec((tm, ROPE_HALF), lambda i, j: (i % tab_tiles, 0))
    qk = _matmul(h, [(w_in, off_qk)], [(cos_t, tab_spec), (sin_t, tab_spec)],
                 functools.partial(_epi_rotary, tn=tn, n_q_tiles=d_qk // tn),
                 2 * d_qk, BF16, tm, tn, "in_proj_qk")
    v = _matmul(h, [(w_in, off_v)], [], _epi_identity, d_v, BF16, tm, tn, "in_proj_v")
    sg = _matmul(h, [(w_in, off_sg)], [], _epi_silu, d_v, BF16, tm, tn, "in_proj_swish_gate")
    gates = _matmul(h, [(w_in, off_gate)], [(p["b_gate"].reshape(1, 2 * d), _row_spec(tn, 0))],
                    _epi_sigmoid_bias, 2 * d, BF16, tm, tn, "in_proj_merge_gates")
    chunk = min(seq, 256)
    o, s_new = _retention(qk, v, sg, ret_state, p["ret_gn_g"], bsz, seq, chunk)

    y_c = _matmul(c, [(p["w_conv_out"], 0)], [(gates, _tile_spec(tm, tn, 0))],
                  _epi_gate_mul, d, F32, tm, tn, "conv_out_proj")
    mix_in = _matmul(o, [(p["w_ret_out"], 0)],
                     [(gates, _tile_spec(tm, tn, d)), (y_c, _tile_spec(tm, tn, 0))],
                     _epi_gate_mul_add, d, BF16, tm, tn, "ret_out_proj")

    tm_n = min(rows, 512)
    x1, h2 = _matmul_norm_residual(mix_in, p["w_out"], x2, p["norm_mix_post"], p["norm_ffn_pre"],
                                   tm_n, 512, "out_proj_norm")

    d_ff = p["w_ffn_gate"].shape[1]
    f = _matmul(h2, [(p["w_ffn_gate"], 0), (p["w_ffn_up"], 0)], [], _epi_swiglu,
                d_ff, BF16, tm, 512, "ffn_up")
    y = _matmul_norm_residual(f, p["w_ffn_down"], x1, p["norm_ffn_post"], None, tm_n, 512, "ffn_down_norm")
    return y.reshape(bsz, seq, d), new_hist, s_new


def kernel(x_prompt, x_sample, cache_conv, state_ret, norm_mix_pre, norm_mix_post, w_in, b_gate, conv_dw, conv_dw_b, conv_ln_g, conv_ln_b, w_conv_out, ret_gn_g, w_ret_out, w_out, norm_ffn_pre, norm_ffn_post, w_ffn_gate, w_ffn_up, w_ffn_down):
    depth = w_in.shape[0]
    past_len = 1024
    y_prompt, y_sample = x_prompt, x_sample
    conv_p, ret_p, conv_s, ret_s = [], [], [], []
    n_batch = x_prompt.shape[0]
    for l in range(depth):
        p = {
            "norm_mix_pre": norm_mix_pre[l], "norm_mix_post": norm_mix_post[l],
            "w_in": w_in[l].astype(BF16), "b_gate": b_gate[l],
            "conv_dw": conv_dw[l], "conv_dw_b": conv_dw_b[l],
            "conv_ln_g": conv_ln_g[l], "conv_ln_b": conv_ln_b[l],
            "w_conv_out": w_conv_out[l].astype(BF16), "ret_gn_g": ret_gn_g[l],
            "w_ret_out": w_ret_out[l].astype(BF16), "w_out": w_out[l].astype(BF16),
            "norm_ffn_pre": norm_ffn_pre[l], "norm_ffn_post": norm_ffn_post[l],
            "w_ffn_gate": w_ffn_gate[l].astype(BF16), "w_ffn_up": w_ffn_up[l].astype(BF16),
            "w_ffn_down": w_ffn_down[l].astype(BF16),
        }
        zero_hist = jnp.zeros((n_batch, CONV_WIDTH - 1, x_prompt.shape[-1]), x_prompt.dtype)
        zero_state = jnp.zeros((n_batch, N_RET_HEADS, RET_DK, RET_DV), x_prompt.dtype)
        y_prompt, hp, sp = _trunk_layer(y_prompt, zero_hist, zero_state, 0, p)
        y_sample, hs, ss = _trunk_layer(y_sample, cache_conv[l], state_ret[l], past_len, p)
        conv_p.append(hp)
        ret_p.append(sp)
        conv_s.append(hs)
        ret_s.append(ss)
    return (y_prompt, y_sample, jnp.stack(conv_p), jnp.stack(ret_p), jnp.stack(conv_s), jnp.stack(ret_s))
```

```python
import functools
import math

import jax
import jax.numpy as jnp
from jax import lax
from jax.experimental import pallas as pl
from jax.experimental.pallas import tpu as pltpu

F32 = jnp.float32
BF16 = jnp.bfloat16

EPS = 1e-6
ROPE_BASE = 10000.0
CONV_WIDTH = 31
N_RET_HEADS = 8
RET_DK = 256
RET_DV = 512
ROPE_HALF = RET_DK // 2

MIB = 1024 * 1024
V7X_VMEM_REQUEST_CAP = 56 * MIB
SUBLANES = 8
CONV_HALO_ROWS = 32


def _params(est_bytes):
    limit = min(V7X_VMEM_REQUEST_CAP, max(32 * MIB, int(est_bytes * 1.25)))
    return pltpu.CompilerParams(vmem_limit_bytes=limit)


def _nbytes(shape, dtype):
    return math.prod(shape) * jnp.dtype(dtype).itemsize


def _sigmoid(x):
    return jax.nn.sigmoid(x)


def _silu(x):
    return x * jax.nn.sigmoid(x)


def _rms_kernel(x_ref, w_ref, o_ref):
    x = x_ref[...]
    ms = jnp.mean(x * x, axis=-1, keepdims=True)
    o_ref[...] = (x * lax.rsqrt(ms + EPS) * w_ref[...]).astype(o_ref.dtype)


def _rms_norm_bf16(x, w, tm):
    rows, d = x.shape
    est = 2 * tm * d * 4 + 2 * tm * d * 2
    return pl.pallas_call(
        _rms_kernel,
        grid=(rows // tm,),
        in_specs=[pl.BlockSpec((tm, d), lambda i: (i, 0)),
                  pl.BlockSpec((1, d), lambda i: (0, 0))],
        out_specs=pl.BlockSpec((tm, d), lambda i: (i, 0)),
        out_shape=jax.ShapeDtypeStruct((rows, d), BF16),
        compiler_params=_params(est),
        name="rms_norm",
    )(x, w.reshape(1, d))


def _mm_kernel(*refs, n_w, epilogue):
    a_ref = refs[0]
    w_refs = refs[1:1 + n_w]
    extra_refs = refs[1 + n_w:-1]
    o_ref = refs[-1]
    a = a_ref[...]
    accs = [jnp.dot(a, w_ref[...], preferred_element_type=F32) for w_ref in w_refs]
    o_ref[...] = epilogue(accs, extra_refs).astype(o_ref.dtype)


def _matmul(a, ws, extras, epilogue, n_out, out_dtype, tm, tn, name):
    rows, k = a.shape
    grid = (rows // tm, n_out // tn)
    in_specs = [pl.BlockSpec((tm, k), lambda i, j: (i, 0))]
    args = [a]
    est = 2 * tm * k * 2
    for w, off in ws:
        assert off % tn == 0
        in_specs.append(pl.BlockSpec((k, tn), functools.partial(lambda i, j, o: (0, o + j), o=off // tn)))
        args.append(w)
        est += 2 * k * tn * 2
    for arr, spec in extras:
        in_specs.append(spec)
        args.append(arr)
        est += 2 * _nbytes(spec.block_shape, arr.dtype)
    est += 2 * tm * tn * jnp.dtype(out_dtype).itemsize
    est += (len(ws) + 1) * tm * tn * 4
    return pl.pallas_call(
        functools.partial(_mm_kernel, n_w=len(ws), epilogue=epilogue),
        grid=grid,
        in_specs=in_specs,
        out_specs=pl.BlockSpec((tm, tn), lambda i, j: (i, j)),
        out_shape=jax.ShapeDtypeStruct((rows, n_out), out_dtype),
        compiler_params=_params(est),
        name=name,
    )(*args)


def _tile_spec(tm, tn, off):
    return pl.BlockSpec((tm, tn), functools.partial(lambda i, j, o: (i, o + j), o=off // tn))


def _row_spec(tn, off):
    return pl.BlockSpec((1, tn), functools.partial(lambda i, j, o: (0, o + j), o=off // tn))


def _epi_glu(accs, extras):
    return accs[0] * _sigmoid(accs[1])


def _epi_identity(accs, extras):
    return accs[0]


def _epi_silu(accs, extras):
    return _silu(accs[0])


def _epi_sigmoid_bias(accs, extras):
    return _sigmoid(accs[0] + extras[0][...])


def _epi_rotary(accs, extras, *, tn, n_q_tiles):
    cos = extras[0][...]
    sin = extras[1][...]
    scale = jnp.where(pl.program_id(1) >= n_q_tiles, RET_DK ** -0.5, 1.0).astype(F32)
    acc = accs[0]
    outs = []
    for h in range(tn // RET_DK):
        x1 = acc[:, h * RET_DK:h * RET_DK + ROPE_HALF]
        x2 = acc[:, h * RET_DK + ROPE_HALF:(h + 1) * RET_DK]
        outs.append((x1 * cos - x2 * sin) * scale)
        outs.append((x1 * sin + x2 * cos) * scale)
    return jnp.concatenate(outs, axis=-1)


def _epi_gate_mul(accs, extras):
    return extras[0][...].astype(F32) * accs[0]


def _epi_gate_mul_add(accs, extras):
    return extras[0][...].astype(F32) * accs[0] + extras[1][...]


def _epi_swiglu(accs, extras):
    return _silu(accs[0]) * accs[1]


def _rope_kernel(inv_ref, cos_ref, sin_ref, *, start, seq_len, tt):
    row = pl.program_id(0) * tt + lax.broadcasted_iota(jnp.int32, (tt, ROPE_HALF), 0)
    pos = (start + lax.rem(row, seq_len)).astype(F32)
    ang = pos * inv_ref[...]
    cos_ref[...] = jnp.cos(ang)
    sin_ref[...] = jnp.sin(ang)


def _rope_tables(start, seq_len, n_rows, tt):
    inv_freq = 1.0 / (ROPE_BASE ** (jnp.arange(ROPE_HALF, dtype=F32) / ROPE_HALF))
    shape = jax.ShapeDtypeStruct((n_rows, ROPE_HALF), F32)
    return pl.pallas_call(
        functools.partial(_rope_kernel, start=start, seq_len=seq_len, tt=tt),
        grid=(n_rows // tt,),
        in_specs=[pl.BlockSpec((1, ROPE_HALF), lambda i: (0, 0))],
        out_specs=[pl.BlockSpec((tt, ROPE_HALF), lambda i: (i, 0))] * 2,
        out_shape=[shape, shape],
        name="rope_tables",
    )(inv_freq.reshape(1, ROPE_HALF))


def _conv_taps(x, w_ref, b_ref, cs, n_rows):
    lead = CONV_HALO_ROWS - (CONV_WIDTH - 1)
    acc = None
    for r in range(SUBLANES):
        rows = n_rows if r == 0 else n_rows + SUBLANES
        g = None
        for p in range((CONV_WIDTH + lead + SUBLANES - 1) // SUBLANES):
            j = SUBLANES * p + r - lead
            if 0 <= j < CONV_WIDTH:
                term = w_ref[j:j + 1, cs] * x[SUBLANES * p:SUBLANES * p + rows]
                g = term if g is None else g + term
        shifted = g[r:r + n_rows]
        acc = shifted if acc is None else acc + shifted
    return acc + b_ref[:, cs]


def _conv_kernel(cur_ref, prev_ref, hist_ref, w_ref, b_ref, g_ref, beta_ref, o_ref,
                 win_ref, pre_ref, *, tt, row_blk, col_blk):
    n_hist = CONV_WIDTH - 1
    lead = CONV_HALO_ROWS - n_hist
    t = pl.program_id(1)

    @pl.when(t == 0)
    def _():
        win_ref[lead:CONV_HALO_ROWS, :] = hist_ref[0]

    @pl.when(t > 0)
    def _():
        win_ref[0:CONV_HALO_ROWS, :] = prev_ref[0]

    win_ref[CONV_HALO_ROWS:CONV_HALO_ROWS + tt, :] = cur_ref[0]

    d = cur_ref.shape[-1]
    for cb in range(d // col_blk):
        cs = slice(cb * col_blk, (cb + 1) * col_blk)
        for rb in range(tt // row_blk):
            x = win_ref[rb * row_blk:rb * row_blk + row_blk + CONV_HALO_ROWS, cs]
            pre_ref[rb * row_blk:(rb + 1) * row_blk, cs] = _conv_taps(x, w_ref, b_ref, cs, row_blk)

    c = pre_ref[...]
    mu = jnp.mean(c, axis=-1, keepdims=True)
    cc = c - mu
    var = jnp.mean(cc * cc, axis=-1, keepdims=True)
    y = cc * lax.rsqrt(var + EPS) * g_ref[...] + beta_ref[...]
    o_ref[0] = _silu(y).astype(o_ref.dtype)


def _conv_module(a, hist, conv_dw, conv_dw_b, ln_g, ln_b, tt):
    bsz, seq, d = a.shape
    n_hist = CONV_WIDTH - 1
    halo_per_tile = tt // CONV_HALO_ROWS
    row_blk = min(tt, 64)
    col_blk = 128
    est = (2 * tt * d * 4 + 2 * CONV_HALO_ROWS * d * 4 + 2 * CONV_HALO_ROWS * d * 4
           + 2 * tt * d * 2 + (tt + CONV_HALO_ROWS) * d * 4 + 3 * tt * d * 4)
    vec = lambda v: v.reshape(1, d)
    const2 = lambda b, t: (0, 0)
    return pl.pallas_call(
        functools.partial(_conv_kernel, tt=tt, row_blk=row_blk, col_blk=col_blk),
        grid=(bsz, seq // tt),
        in_specs=[
            pl.BlockSpec((1, tt, d), lambda b, t: (b, t, 0)),
            pl.BlockSpec((1, CONV_HALO_ROWS, d),
                         lambda b, t: (b, jnp.maximum(t * halo_per_tile - 1, 0), 0)),
            pl.BlockSpec((1, n_hist, d), lambda b, t: (b, 0, 0)),
            pl.BlockSpec((CONV_WIDTH, d), const2),
            pl.BlockSpec((1, d), const2),
            pl.BlockSpec((1, d), const2),
            pl.BlockSpec((1, d), const2),
        ],
        out_specs=pl.BlockSpec((1, tt, d), lambda b, t: (b, t, 0)),
        out_shape=jax.ShapeDtypeStruct((bsz, seq, d), BF16),
        scratch_shapes=[pltpu.VMEM((CONV_HALO_ROWS + tt, d), F32),
                        pltpu.VMEM((tt, d), F32)],
        compiler_params=_params(est),
        name="conv_module",
    )(a, a, hist, conv_dw, vec(conv_dw_b), vec(ln_g), vec(ln_b))


def _log_decay(h):
    return math.log1p(-(2.0 ** (-5.0 - h)))


def _retention_kernel(q_ref, k_ref, v_ref, sg_ref, s0_ref, gn_ref, o_ref, s_out_ref,
                      s_ref, decay_ref, *, chunk, n_chunks):
    b = pl.program_id(0)
    c = pl.program_id(1)

    @pl.when((b == 0) & (c == 0))
    def _():
        n = lax.broadcasted_iota(jnp.int32, (chunk, chunk), 0)
        m = lax.broadcasted_iota(jnp.int32, (chunk, chunk), 1)
        diff = (n - m).astype(F32)
        for h in range(N_RET_HEADS):
            decay_ref[h] = jnp.where(diff >= 0.0, jnp.exp(jnp.maximum(diff, 0.0) * _log_decay(h)), 0.0)

    @pl.when(c == 0)
    def _():
        s_ref[...] = s0_ref[0]

    idx = lax.broadcasted_iota(jnp.int32, (chunk, 1), 0).astype(F32)
    for h in range(N_RET_HEADS):
        lg = _log_decay(h)
        q = q_ref[:, h * RET_DK:(h + 1) * RET_DK]
        k = k_ref[:, h * RET_DK:(h + 1) * RET_DK]
        v = v_ref[:, h * RET_DV:(h + 1) * RET_DV]
        s_old = s_ref[h]
        scores = lax.dot_general(q, k, (((1,), (1,)), ((), ())), preferred_element_type=F32)
        scores = (scores * decay_ref[h]).astype(BF16)
        o = jnp.dot(scores, v, preferred_element_type=F32)
        xi = jnp.exp((idx + 1.0) * lg)
        o = o + jnp.dot(q, s_old.astype(BF16), preferred_element_type=F32) * xi
        zeta = jnp.exp((chunk - 1.0 - idx) * lg)
        kz = (k.astype(F32) * zeta).astype(BF16)
        s_ref[h] = math.exp(chunk * lg) * s_old + lax.dot_general(
            kz, v, (((0,), (0,)), ((), ())), preferred_element_type=F32)
        mu = jnp.mean(o, axis=-1, keepdims=True)
        oc = o - mu
        var = jnp.mean(oc * oc, axis=-1, keepdims=True)
        on = oc * lax.rsqrt(var + EPS) * gn_ref[:, h * RET_DV:(h + 1) * RET_DV]
        gate = sg_ref[:, h * RET_DV:(h + 1) * RET_DV].astype(F32)
        o_ref[:, h * RET_DV:(h + 1) * RET_DV] = (on * gate).astype(o_ref.dtype)

    @pl.when(c == n_chunks - 1)
    def _():
        s_out_ref[0] = s_ref[...]


def _retention(qk, v, sg, state, gn_g, bsz, seq, chunk):
    rows = bsz * seq
    n_chunks = seq // chunk
    dqk = N_RET_HEADS * RET_DK
    dv = N_RET_HEADS * RET_DV
    state_blk = (1, N_RET_HEADS, RET_DK, RET_DV)
    est = (2 * 2 * chunk * dqk * 2 + 3 * 2 * chunk * dv * 2 + 4 * _nbytes(state_blk, F32)
           + _nbytes(state_blk, F32) + N_RET_HEADS * chunk * chunk * 4 + 8 * chunk * RET_DV * 4)
    row_blk = lambda b, c: (b * n_chunks + c, 0)
    return pl.pallas_call(
        functools.partial(_retention_kernel, chunk=chunk, n_chunks=n_chunks),
        grid=(bsz, n_chunks),
        in_specs=[
            pl.BlockSpec((chunk, dqk), row_blk),
            pl.BlockSpec((chunk, dqk), lambda b, c: (b * n_chunks + c, 1)),
            pl.BlockSpec((chunk, dv), row_blk),
            pl.BlockSpec((chunk, dv), row_blk),
            pl.BlockSpec(state_blk, lambda b, c: (b, 0, 0, 0)),
            pl.BlockSpec((1, dv), lambda b, c: (0, 0)),
        ],
        out_specs=[pl.BlockSpec((chunk, dv), row_blk),
                   pl.BlockSpec(state_blk, lambda b, c: (b, 0, 0, 0))],
        out_shape=[jax.ShapeDtypeStruct((rows, dv), BF16),
                   jax.ShapeDtypeStruct((bsz,) + state_blk[1:], F32)],
        scratch_shapes=[pltpu.VMEM(state_blk[1:], F32),
                        pltpu.VMEM((N_RET_HEADS, chunk, chunk), F32)],
        compiler_params=_params(est),
        name="retention",
    )(qk, qk, v, sg, state, gn_g.reshape(1, dv))


def _mm_norm_kernel(*refs, n_j, emit_next, sub):
    if emit_next:
        a_ref, w_ref, res_ref, nw_ref, nw2_ref, o_ref, h_ref, acc_ref = refs
    else:
        a_ref, w_ref, res_ref, nw_ref, o_ref, acc_ref = refs
    j = pl.program_id(1)
    tm = a_ref.shape[0]
    tn = w_ref.shape[-1]
    d = n_j * tn

    if n_j > 1:
        @pl.when(j < n_j - 1)
        def _():
            acc_ref[j] = jnp.dot(a_ref[...], w_ref[...], preferred_element_type=F32)

    @pl.when(j == n_j - 1)
    def _():
        for s in range(tm // sub):
            rs = slice(s * sub, (s + 1) * sub)
            parts = [acc_ref[jj, rs, :] for jj in range(n_j - 1)]
            parts.append(jnp.dot(a_ref[rs, :], w_ref[...], preferred_element_type=F32))
            ss = sum(jnp.sum(m * m, axis=-1, keepdims=True) for m in parts)
            inv = lax.rsqrt(ss / d + EPS)
            ss2 = 0.0
            for jj in range(n_j):
                cs = slice(jj * tn, (jj + 1) * tn)
                y = res_ref[rs, cs] + parts[jj] * inv * nw_ref[:, cs]
                o_ref[rs, cs] = y
                if emit_next:
                    ss2 = ss2 + jnp.sum(y * y, axis=-1, keepdims=True)
            if emit_next:
                inv2 = lax.rsqrt(ss2 / d + EPS)
                for jj in range(n_j):
                    cs = slice(jj * tn, (jj + 1) * tn)
                    h_ref[rs, cs] = (o_ref[rs, cs] * inv2 * nw2_ref[:, cs]).astype(h_ref.dtype)


def _matmul_norm_residual(a, w, res, norm_w, next_norm_w, tm, tn, name):
    rows, k = a.shape
    d = w.shape[1]
    n_j = d // tn
    emit_next = next_norm_w is not None
    row_i = lambda i, j: (i, 0)
    const2 = lambda i, j: (0, 0)
    in_specs = [pl.BlockSpec((tm, k), row_i),
                pl.BlockSpec((k, tn), lambda i, j: (0, j)),
                pl.BlockSpec((tm, d), row_i),
                pl.BlockSpec((1, d), const2)]
    args = [a, w, res, norm_w.reshape(1, d)]
    out_specs = [pl.BlockSpec((tm, d), row_i)]
    out_shape = [jax.ShapeDtypeStruct((rows, d), F32)]
    est = 2 * tm * k * 2 + 2 * k * tn * 2 + 2 * tm * d * 4 + 2 * tm * d * 4 + tm * d * 4 + 2 * tm * tn * 4
    if emit_next:
        in_specs.append(pl.BlockSpec((1, d), const2))
        args.append(next_norm_w.reshape(1, d))
        out_specs.append(pl.BlockSpec((tm, d), row_i))
        out_shape.append(jax.ShapeDtypeStruct((rows, d), BF16))
        est += 2 * tm * d * 2
    outs = pl.pallas_call(
        functools.partial(_mm_norm_kernel, n_j=n_j, emit_next=emit_next, sub=min(tm, 128)),
        grid=(rows // tm, n_j),
        in_specs=in_specs,
        out_specs=out_specs,
        out_shape=out_shape,
        scratch_shapes=[pltpu.VMEM((max(n_j - 1, 1),) + ((tm, tn) if n_j > 1 else (SUBLANES, 128)), F32)],
        compiler_params=_params(est),
        name=name,
    )(*args)
    return outs if emit_next else outs[0]


def _trunk_layer(x, conv_hist, ret_state, start, p):
    bsz, seq, d = x.shape
    rows = bsz * seq
    x2 = x.reshape(rows, d)
    d_qk = N_RET_HEADS * RET_DK
    d_v = N_RET_HEADS * RET_DV
    off_glu_v, off_glu_g = 0, d
    off_qk = 2 * d
    off_v = off_qk + 2 * d_qk
    off_sg = off_v + d_v
    off_gate = off_sg + d_v

    tm = min(rows, 1024)
    tn = 1024
    w_in = p["w_in"]

    h = _rms_norm_bf16(x2, p["norm_mix_pre"], min(rows, 512))

    a = _matmul(h, [(w_in, off_glu_v), (w_in, off_glu_g)], [], _epi_glu, d, F32, tm, 512, "in_proj_glu")
    a3 = a.reshape(bsz, seq, d)
    new_hist = a3[:, seq - (CONV_WIDTH - 1):, :]
    conv_tt = min(seq, 128)
    c = _conv_module(a3, conv_hist, p["conv_dw"], p["conv_dw_b"], p["conv_ln_g"], p["conv_ln_b"], conv_tt)
    c = c.reshape(rows, d)

    n_tab = max(seq, tm)
    cos_t, sin_t = _rope_tables(start, seq, n_tab, min(n_tab, 1024))
    tab_tiles = n_tab // tm
    tab_spec = pl.BlockSpec((tm, ROPE_HALF), lambda i, j: (i % tab_tiles, 0))
    qk = _matmul(h, [(w_in, off_qk)], [(cos_t, tab_spec), (sin_t, tab_spec)],
                 functools.partial(_epi_rotary, tn=tn, n_q_tiles=d_qk // tn),
                 2 * d_qk, BF16, tm, tn, "in_proj_qk")
    v = _matmul(h, [(w_in, off_v)], [], _epi_identity, d_v, BF16, tm, tn, "in_proj_v")
    sg = _matmul(h, [(w_in, off_sg)], [], _epi_silu, d_v, BF16, tm, tn, "in_proj_swish_gate")
    gates = _matmul(h, [(w_in, off_gate)], [(p["b_gate"].reshape(1, 2 * d), _row_spec(tn, 0))],
                    _epi_sigmoid_bias, 2 * d, BF16, tm, tn, "in_proj_merge_gates")
    chunk = min(seq, 256)
    o, s_new = _retention(qk, v, sg, ret_state, p["ret_gn_g"], bsz, seq, chunk)

    y_c = _matmul(c, [(p["w_conv_out"], 0)], [(gates, _tile_spec(tm, tn, 0))],
                  _epi_gate_mul, d, F32, tm, tn, "conv_out_proj")
    mix_in = _matmul(o, [(p["w_ret_out"], 0)],
                     [(gates, _tile_spec(tm, tn, d)), (y_c, _tile_spec(tm, tn, 0))],
                     _epi_gate_mul_add, d, BF16, tm, tn, "ret_out_proj")

    tm_n = min(rows, 512)
    x1, h2 = _matmul_norm_residual(mix_in, p["w_out"], x2, p["norm_mix_post"], p["norm_ffn_pre"],
                                   tm_n, d, "out_proj_norm")

    d_ff = p["w_ffn_gate"].shape[1]
    f = _matmul(h2, [(p["w_ffn_gate"], 0), (p["w_ffn_up"], 0)], [], _epi_swiglu,
                d_ff, BF16, tm, 512, "ffn_up")
    y = _matmul_norm_residual(f, p["w_ffn_down"], x1, p["norm_ffn_post"], None, tm_n, 512, "ffn_down_norm")
    return y.reshape(bsz, seq, d), new_hist, s_new


def kernel(x_prompt, x_sample, cache_conv, state_ret, norm_mix_pre, norm_mix_post, w_in, b_gate, conv_dw, conv_dw_b, conv_ln_g, conv_ln_b, w_conv_out, ret_gn_g, w_ret_out, w_out, norm_ffn_pre, norm_ffn_post, w_ffn_gate, w_ffn_up, w_ffn_down):
    depth = w_in.shape[0]
    past_len = 1024
    y_prompt, y_sample = x_prompt, x_sample
    conv_p, ret_p, conv_s, ret_s = [], [], [], []
    n_batch = x_prompt.shape[0]
    for l in range(depth):
        p = {
            "norm_mix_pre": norm_mix_pre[l], "norm_mix_post": norm_mix_post[l],
            "w_in": w_in[l].astype(BF16), "b_gate": b_gate[l],
            "conv_dw": conv_dw[l], "conv_dw_b": conv_dw_b[l],
            "conv_ln_g": conv_ln_g[l], "conv_ln_b": conv_ln_b[l],
            "w_conv_out": w_conv_out[l].astype(BF16), "ret_gn_g": ret_gn_g[l],
            "w_ret_out": w_ret_out[l].astype(BF16), "w_out": w_out[l].astype(BF16),
            "norm_ffn_pre": norm_ffn_pre[l], "norm_ffn_post": norm_ffn_post[l],
            "w_ffn_gate": w_ffn_gate[l].astype(BF16), "w_ffn_up": w_ffn_up[l].astype(BF16),
            "w_ffn_down": w_ffn_down[l].astype(BF16),
        }
        zero_hist = jnp.zeros((n_batch, CONV_WIDTH - 1, x_prompt.shape[-1]), x_prompt.dtype)
        zero_state = jnp.zeros((n_batch, N_RET_HEADS, RET_DK, RET_DV), x_prompt.dtype)
        y_prompt, hp, sp = _trunk_layer(y_prompt, zero_hist, zero_state, 0, p)
        y_sample, hs, ss = _trunk_layer(y_sample, cache_conv[l], state_ret[l], past_len, p)
        conv_p.append(hp)
        ret_p.append(sp)
        conv_s.append(hs)
        ret_s.append(ss)
    return (y_prompt, y_sample, jnp.stack(conv_p), jnp.stack(ret_p), jnp.stack(conv_s), jnp.stack(ret_s))
```

```python
import functools
import math

import jax
import jax.numpy as jnp
from jax import lax
from jax.experimental import pallas as pl
from jax.experimental.pallas import tpu as pltpu

F32 = jnp.float32
BF16 = jnp.bfloat16

EPS = 1e-6
ROPE_BASE = 10000.0
CONV_WIDTH = 31
N_RET_HEADS = 8
RET_DK = 256
RET_DV = 512
ROPE_HALF = RET_DK // 2

MIB = 1024 * 1024
V7X_VMEM_REQUEST_CAP = 56 * MIB
SUBLANES = 8
LANES = 128
CONV_HALO_ROWS = 32
CONV_LEAD = CONV_HALO_ROWS - (CONV_WIDTH - 1)
CONV_ROW_BLK = 32
NORM_SUB_ROWS = 128
MM_SUB_ROWS = 256
IN_PROJ_TILE = 1024


def _params(est_bytes):
    limit = min(V7X_VMEM_REQUEST_CAP, max(32 * MIB, int(est_bytes * 1.25)))
    return pltpu.CompilerParams(vmem_limit_bytes=limit)


def _nbytes(shape, dtype):
    return math.prod(shape) * jnp.dtype(dtype).itemsize


def _sigmoid(x):
    return jax.nn.sigmoid(x)


def _silu(x):
    return x * jax.nn.sigmoid(x)


def _dot(a, b):
    return jnp.dot(a, b, preferred_element_type=F32)


def _rms_kernel(x_ref, w_ref, o_ref):
    x = x_ref[...]
    ms = jnp.mean(x * x, axis=-1, keepdims=True)
    o_ref[...] = (x * lax.rsqrt(ms + EPS) * w_ref[...]).astype(o_ref.dtype)


def _rms_norm_bf16(x, w, tm):
    rows, d = x.shape
    est = 2 * tm * d * 4 + 2 * tm * d * 2
    return pl.pallas_call(
        _rms_kernel,
        grid=(rows // tm,),
        in_specs=[pl.BlockSpec((tm, d), lambda i: (i, 0)),
                  pl.BlockSpec((1, d), lambda i: (0, 0))],
        out_specs=pl.BlockSpec((tm, d), lambda i: (i, 0)),
        out_shape=jax.ShapeDtypeStruct((rows, d), BF16),
        compiler_params=_params(est),
        name="rms_norm",
    )(x, w.reshape(1, d))


def _mm_kernel(*refs, n_w, epilogue, sub):
    a_ref = refs[0]
    w_refs = refs[1:1 + n_w]
    extra_refs = refs[1 + n_w:-1]
    o_ref = refs[-1]
    for s in range(a_ref.shape[0] // sub):
        rs = slice(s * sub, (s + 1) * sub)
        a = a_ref[rs, :]
        accs = [_dot(a, w_ref[...]) for w_ref in w_refs]
        o_ref[rs, :] = epilogue(accs, extra_refs, rs).astype(o_ref.dtype)


def _matmul(a, ws, extras, epilogue, n_out, out_dtype, tm, tn, name):
    rows, k = a.shape
    grid = (rows // tm, n_out // tn)
    in_specs = [pl.BlockSpec((tm, k), lambda i, j: (i, 0))]
    args = [a]
    est = 2 * tm * k * 2
    for w, off in ws:
        assert off % tn == 0
        in_specs.append(pl.BlockSpec((k, tn), functools.partial(lambda i, j, o: (0, o + j), o=off // tn)))
        args.append(w)
        est += 2 * k * tn * 2
    for arr, spec in extras:
        in_specs.append(spec)
        args.append(arr)
        est += 2 * _nbytes(spec.block_shape, arr.dtype)
    est += 2 * tm * tn * jnp.dtype(out_dtype).itemsize
    est += (len(ws) + 1) * tm * tn * 4
    return pl.pallas_call(
        functools.partial(_mm_kernel, n_w=len(ws), epilogue=epilogue, sub=min(tm, MM_SUB_ROWS)),
        grid=grid,
        in_specs=in_specs,
        out_specs=pl.BlockSpec((tm, tn), lambda i, j: (i, j)),
        out_shape=jax.ShapeDtypeStruct((rows, n_out), out_dtype),
        compiler_params=_params(est),
        name=name,
    )(*args)


def _tile_spec(tm, tn, off):
    assert off % tn == 0
    return pl.BlockSpec((tm, tn), functools.partial(lambda i, j, o: (i, o + j), o=off // tn))


def _row_spec(tn, off):
    return pl.BlockSpec((1, tn), functools.partial(lambda i, j, o: (0, o + j), o=off // tn))


def _rotary(acc, cos, sin, scale):
    outs = []
    for h in range(acc.shape[-1] // RET_DK):
        x1 = acc[:, h * RET_DK:h * RET_DK + ROPE_HALF]
        x2 = acc[:, h * RET_DK + ROPE_HALF:(h + 1) * RET_DK]
        outs.append((x1 * cos - x2 * sin) * scale)
        outs.append((x1 * sin + x2 * cos) * scale)
    return jnp.concatenate(outs, axis=-1)


def _epi_glu(accs, extras, rs):
    return accs[0] * _sigmoid(accs[1])


def _epi_identity(accs, extras, rs):
    return accs[0]


def _epi_silu(accs, extras, rs):
    return _silu(accs[0])


def _epi_sigmoid_bias(accs, extras, rs):
    return _sigmoid(accs[0] + extras[0][...])


def _epi_rotary(accs, extras, rs, *, n_q_tiles):
    scale = jnp.where(pl.program_id(1) >= n_q_tiles, RET_DK ** -0.5, 1.0).astype(F32)
    return _rotary(accs[0], extras[0][rs, :], extras[1][rs, :], scale)


def _epi_gate_mul_add(accs, extras, rs):
    return extras[0][rs, :].astype(F32) * accs[0] + extras[1][rs, :]


def _epi_swiglu(accs, extras, rs):
    return _silu(accs[0]) * accs[1]


def _rope_kernel(inv_ref, cos_ref, sin_ref, *, start, seq_len, tt):
    row = pl.program_id(0) * tt + lax.broadcasted_iota(jnp.int32, (tt, ROPE_HALF), 0)
    pos = (start + lax.rem(row, seq_len)).astype(F32)
    ang = pos * inv_ref[...]
    cos_ref[...] = jnp.cos(ang)
    sin_ref[...] = jnp.sin(ang)


def _rope_tables(start, seq_len, n_rows, tt):
    inv_freq = 1.0 / (ROPE_BASE ** (jnp.arange(ROPE_HALF, dtype=F32) / ROPE_HALF))
    shape = jax.ShapeDtypeStruct((n_rows, ROPE_HALF), F32)
    return pl.pallas_call(
        functools.partial(_rope_kernel, start=start, seq_len=seq_len, tt=tt),
        grid=(n_rows // tt,),
        in_specs=[pl.BlockSpec((1, ROPE_HALF), lambda i: (0, 0))],
        out_specs=[pl.BlockSpec((tt, ROPE_HALF), lambda i: (i, 0))] * 2,
        out_shape=[shape, shape],
        name="rope_tables",
    )(inv_freq.reshape(1, ROPE_HALF))


def _conv_taps(x, w_row, bias, n_rows):
    acc = None
    for r in range(SUBLANES):
        rows = n_rows if r == 0 else n_rows + SUBLANES
        g = None
        for p in range((CONV_WIDTH + CONV_LEAD + SUBLANES - 1) // SUBLANES):
            j = SUBLANES * p + r - CONV_LEAD
            if 0 <= j < CONV_WIDTH:
                term = w_row(j) * x[SUBLANES * p:SUBLANES * p + rows]
                g = term if g is None else g + term
        shifted = g[r:r + n_rows]
        acc = shifted if acc is None else acc + shifted
    return acc + bias


def _conv_kernel(cur_ref, prev_ref, hist_ref, w_ref, b_ref, o_ref, win_ref, *, tt, row_blk):
    t = pl.program_id(1)

    @pl.when(t == 0)
    def _():
        win_ref[CONV_LEAD:CONV_HALO_ROWS, :] = hist_ref[0]

    @pl.when(t > 0)
    def _():
        win_ref[0:CONV_HALO_ROWS, :] = prev_ref[0]

    win_ref[CONV_HALO_ROWS:CONV_HALO_ROWS + tt, :] = cur_ref[0]

    d = cur_ref.shape[-1]
    for cb in range(d // LANES):
        cs = slice(cb * LANES, (cb + 1) * LANES)
        for rb in range(tt // row_blk):
            x = win_ref[rb * row_blk:rb * row_blk + row_blk + CONV_HALO_ROWS, cs]
            o_ref[0, rb * row_blk:(rb + 1) * row_blk, cs] = _conv_taps(
                x, lambda j: w_ref[j:j + 1, cs], b_ref[:, cs], row_blk)


def _conv_module(a, hist, conv_dw, conv_dw_b, tt):
    bsz, seq, d = a.shape
    n_hist = CONV_WIDTH - 1
    halo_per_tile = tt // CONV_HALO_ROWS
    row_blk = min(tt, CONV_ROW_BLK)
    est = 2 * tt * d * 4 + 4 * CONV_HALO_ROWS * d * 4 + 2 * tt * d * 4 + (tt + CONV_HALO_ROWS) * d * 4
    const2 = lambda b, t: (0, 0)
    return pl.pallas_call(
        functools.partial(_conv_kernel, tt=tt, row_blk=row_blk),
        grid=(bsz, seq // tt),
        in_specs=[
            pl.BlockSpec((1, tt, d), lambda b, t: (b, t, 0)),
            pl.BlockSpec((1, CONV_HALO_ROWS, d),
                         lambda b, t: (b, jnp.maximum(t * halo_per_tile - 1, 0), 0)),
            pl.BlockSpec((1, n_hist, d), lambda b, t: (b, 0, 0)),
            pl.BlockSpec((CONV_WIDTH, d), const2),
            pl.BlockSpec((1, d), const2),
        ],
        out_specs=pl.BlockSpec((1, tt, d), lambda b, t: (b, t, 0)),
        out_shape=jax.ShapeDtypeStruct((bsz, seq, d), F32),
        scratch_shapes=[pltpu.VMEM((CONV_HALO_ROWS + tt, d), F32)],
        compiler_params=_params(est),
        name="conv_module",
    )(a, a, hist, conv_dw, conv_dw_b.reshape(1, d))


def _in_proj_conv_kernel(h_ref, h2_ref, w_ref, acur_ref, aprev_ref, hist_ref, cw_ref, cb_ref, cos_ref, sin_ref,
                         bg_ref, cpre_ref, big_ref, win_ref, *, tm, tiles_per_seq, bounds):
    i = pl.program_id(0)
    t = pl.program_id(1)
    k_start, v_start, sg_start, gate_start = bounds
    n_sub = tm // MM_SUB_ROWS
    conv_blks_per_sub = (tm // CONV_ROW_BLK) // n_sub

    @pl.when(i % tiles_per_seq == 0)
    def _():
        win_ref[CONV_LEAD:CONV_HALO_ROWS, :] = hist_ref[0]

    @pl.when(i % tiles_per_seq != 0)
    def _():
        win_ref[0:CONV_HALO_ROWS, :] = aprev_ref[...]

    win_ref[CONV_HALO_ROWS:CONV_HALO_ROWS + tm, :] = acur_ref[...]

    def step(epilogue):
        half = w_ref.shape[-1] // 2
        conv_blk = 0
        for s in range(n_sub):
            rs = slice(s * MM_SUB_ROWS, (s + 1) * MM_SUB_ROWS)
            for c, h_copy in enumerate((h_ref, h2_ref)):
                cs = slice(c * half, (c + 1) * half)
                acc = _dot(h_copy[rs, :], w_ref[:, cs])
                for _ in range(conv_blks_per_sub // 2):
                    r0 = conv_blk * CONV_ROW_BLK
                    x = win_ref[r0:r0 + CONV_ROW_BLK + CONV_HALO_ROWS, :]
                    cpre_ref[r0:r0 + CONV_ROW_BLK, :] = _conv_taps(
                        x, lambda j: cw_ref[0, j:j + 1, :], cb_ref[0], CONV_ROW_BLK)
                    conv_blk += 1
                big_ref[rs, cs] = epilogue(acc, rs, cs).astype(big_ref.dtype)

    def epi_rotary(acc, rs, cs):
        scale = jnp.where(t >= k_start, RET_DK ** -0.5, 1.0).astype(F32)
        return _rotary(acc, cos_ref[rs, :], sin_ref[rs, :], scale)

    pl.when(t < v_start)(lambda: step(epi_rotary))
    pl.when((t >= v_start) & (t < sg_start))(lambda: step(lambda acc, rs, cs: acc))
    pl.when((t >= sg_start) & (t < gate_start))(lambda: step(lambda acc, rs, cs: _silu(acc)))
    pl.when(t >= gate_start)(lambda: step(lambda acc, rs, cs: _sigmoid(acc + bg_ref[:, cs])))


def _in_proj_conv(h, w_in, a, hist, conv_dw, conv_dw_b, cos_t, sin_t, b_gate, seq, tm):
    rows, d = h.shape
    tn = IN_PROJ_TILE
    n_chunks = d // LANES
    d_qk = N_RET_HEADS * RET_DK
    d_v = N_RET_HEADS * RET_DV
    off_wide = 2 * d
    wide_cols = 2 * d_qk + 2 * d_v + 2 * d
    assert wide_cols // tn == n_chunks and seq % tm == 0 and tm % CONV_HALO_ROWS == 0
    tiles_per_seq = seq // tm
    tab_tiles = cos_t.shape[0] // tm
    halo_per_tile = tm // CONV_HALO_ROWS
    bounds = (d_qk // tn, 2 * d_qk // tn, (2 * d_qk + d_v) // tn, (2 * d_qk + 2 * d_v) // tn)
    n_bias_steps = 2 * d // tn
    chunk_map = lambda i, t: (t, 0, 0)
    tab_map = lambda i, t: (i % tab_tiles, 0)
    cw = conv_dw.reshape(CONV_WIDTH, n_chunks, LANES).transpose(1, 0, 2)
    cb = conv_dw_b.reshape(n_chunks, 1, LANES)
    est = (4 * tm * d * 2 + 2 * d * tn * 2 + 4 * tm * LANES * 4 + 4 * tm * ROPE_HALF * 4
           + 2 * tm * tn * 2 + (tm + CONV_HALO_ROWS) * LANES * 4 + 3 * MM_SUB_ROWS * tn * 4)
    return pl.pallas_call(
        functools.partial(_in_proj_conv_kernel, tm=tm, tiles_per_seq=tiles_per_seq, bounds=bounds),
        grid=(rows // tm, n_chunks),
        in_specs=[
            pl.BlockSpec((tm, d), lambda i, t: (i, 0)),
            pl.BlockSpec((tm, d), lambda i, t: (i, 0)),
            pl.BlockSpec((d, tn), lambda i, t: (0, off_wide // tn + t)),
            pl.BlockSpec((tm, LANES), lambda i, t: (i, t)),
            pl.BlockSpec((CONV_HALO_ROWS, LANES), lambda i, t: (jnp.maximum(i * halo_per_tile - 1, 0), t)),
            pl.BlockSpec((1, CONV_WIDTH - 1, LANES), lambda i, t: (i // tiles_per_seq, 0, t)),
            pl.BlockSpec((1, CONV_WIDTH, LANES), chunk_map),
            pl.BlockSpec((1, 1, LANES), chunk_map),
            pl.BlockSpec((tm, ROPE_HALF), tab_map),
            pl.BlockSpec((tm, ROPE_HALF), tab_map),
            pl.BlockSpec((1, tn), lambda i, t: (0, jnp.clip(t - bounds[3], 0, n_bias_steps - 1))),
        ],
        out_specs=[
            pl.BlockSpec((tm, LANES), lambda i, t: (i, t)),
            pl.BlockSpec((tm, tn), lambda i, t: (i, t)),
        ],
        out_shape=[jax.ShapeDtypeStruct((rows, d), F32),
                   jax.ShapeDtypeStruct((rows, wide_cols), BF16)],
        scratch_shapes=[pltpu.VMEM((CONV_HALO_ROWS + tm, LANES), F32)],
        compiler_params=_params(est),
        name="in_proj_conv",
    )(h, h, w_in, a, a, hist, cw, cb, cos_t, sin_t, b_gate.reshape(1, 2 * d))


def _ln_mm_kernel(c_ref, g_ref, b_ref, w_ref, gate_ref, o_ref, c_scr, *, sub):
    j = pl.program_id(1)
    tm = c_ref.shape[0]

    @pl.when(j == 0)
    def _():
        for s in range(tm // sub):
            rs = slice(s * sub, (s + 1) * sub)
            c = c_ref[rs, :]
            mu = jnp.mean(c, axis=-1, keepdims=True)
            cc = c - mu
            var = jnp.mean(cc * cc, axis=-1, keepdims=True)
            y = cc * lax.rsqrt(var + EPS) * g_ref[...] + b_ref[...]
            act = _silu(y).astype(c_scr.dtype)
            c_scr[rs, :] = act
            o_ref[rs, :] = gate_ref[rs, :].astype(F32) * _dot(act, w_ref[...])

    @pl.when(j > 0)
    def _():
        o_ref[...] = gate_ref[...].astype(F32) * _dot(c_scr[...], w_ref[...])


def _conv_out_proj(c_pre, ln_g, ln_b, w, gate_src, tm, tn):
    rows, d = c_pre.shape
    n_out = w.shape[1]
    gates, gate_off = gate_src
    est = 2 * tm * d * 4 + 2 * d * tn * 2 + 2 * tm * tn * 2 + 2 * tm * tn * 4 + tm * d * 2 + 6 * NORM_SUB_ROWS * d * 4
    const2 = lambda i, j: (0, 0)
    return pl.pallas_call(
        functools.partial(_ln_mm_kernel, sub=min(tm, NORM_SUB_ROWS)),
        grid=(rows // tm, n_out // tn),
        in_specs=[pl.BlockSpec((tm, d), lambda i, j: (i, 0)),
                  pl.BlockSpec((1, d), const2),
                  pl.BlockSpec((1, d), const2),
                  pl.BlockSpec((d, tn), lambda i, j: (0, j)),
                  _tile_spec(tm, tn, gate_off)],
        out_specs=pl.BlockSpec((tm, tn), lambda i, j: (i, j)),
        out_shape=jax.ShapeDtypeStruct((rows, n_out), F32),
        scratch_shapes=[pltpu.VMEM((tm, d), BF16)],
        compiler_params=_params(est),
        name="conv_out_proj",
    )(c_pre, ln_g.reshape(1, d), ln_b.reshape(1, d), w, gates)


def _log_decay(h):
    return math.log1p(-(2.0 ** (-5.0 - h)))


def _retention_kernel(q_ref, k_ref, v_ref, sg_ref, s0_ref, gn_ref, o_ref, s_out_ref,
                      s_ref, decay_ref, *, chunk, n_chunks):
    b = pl.program_id(0)
    c = pl.program_id(1)

    @pl.when((b == 0) & (c == 0))
    def _():
        n = lax.broadcasted_iota(jnp.int32, (chunk, chunk), 0)
        m = lax.broadcasted_iota(jnp.int32, (chunk, chunk), 1)
        diff = (n - m).astype(F32)
        for h in range(N_RET_HEADS):
            decay_ref[h] = jnp.where(diff >= 0.0, jnp.exp(jnp.maximum(diff, 0.0) * _log_decay(h)), 0.0)

    @pl.when(c == 0)
    def _():
        s_ref[...] = s0_ref[0]

    idx = lax.broadcasted_iota(jnp.int32, (chunk, 1), 0).astype(F32)
    for h in range(N_RET_HEADS):
        lg = _log_decay(h)
        q = q_ref[:, h * RET_DK:(h + 1) * RET_DK]
        k = k_ref[:, h * RET_DK:(h + 1) * RET_DK]
        v = v_ref[:, h * RET_DV:(h + 1) * RET_DV]
        s_old = s_ref[h]
        scores = lax.dot_general(q, k, (((1,), (1,)), ((), ())), preferred_element_type=F32)
        scores = (scores * decay_ref[h]).astype(BF16)
        o = _dot(scores, v)
        xi = jnp.exp((idx + 1.0) * lg)
        o = o + _dot(q, s_old.astype(BF16)) * xi
        zeta = jnp.exp((chunk - 1.0 - idx) * lg)
        kz = (k.astype(F32) * zeta).astype(BF16)
        s_ref[h] = math.exp(chunk * lg) * s_old + lax.dot_general(
            kz, v, (((0,), (0,)), ((), ())), preferred_element_type=F32)
        mu = jnp.mean(o, axis=-1, keepdims=True)
        oc = o - mu
        var = jnp.mean(oc * oc, axis=-1, keepdims=True)
        on = oc * lax.rsqrt(var + EPS) * gn_ref[:, h * RET_DV:(h + 1) * RET_DV]
        gate = sg_ref[:, h * RET_DV:(h + 1) * RET_DV].astype(F32)
        o_ref[:, h * RET_DV:(h + 1) * RET_DV] = (on * gate).astype(o_ref.dtype)

    @pl.when(c == n_chunks - 1)
    def _():
        s_out_ref[0] = s_ref[...]


def _retention(q_src, k_src, v_src, sg_src, state, gn_g, bsz, seq, chunk):
    rows = bsz * seq
    n_chunks = seq // chunk
    dqk = N_RET_HEADS * RET_DK
    dv = N_RET_HEADS * RET_DV
    state_blk = (1, N_RET_HEADS, RET_DK, RET_DV)
    est = (2 * 2 * chunk * dqk * 2 + 3 * 2 * chunk * dv * 2 + 4 * _nbytes(state_blk, F32)
           + _nbytes(state_blk, F32) + N_RET_HEADS * chunk * chunk * 4 + 8 * chunk * RET_DV * 4)

    def src_spec(src, width):
        assert src[1] % width == 0
        return pl.BlockSpec((chunk, width), functools.partial(
            lambda b, c, o: (b * n_chunks + c, o), o=src[1] // width))

    return pl.pallas_call(
        functools.partial(_retention_kernel, chunk=chunk, n_chunks=n_chunks),
        grid=(bsz, n_chunks),
        in_specs=[
            src_spec(q_src, dqk), src_spec(k_src, dqk), src_spec(v_src, dv), src_spec(sg_src, dv),
            pl.BlockSpec(state_blk, lambda b, c: (b, 0, 0, 0)),
            pl.BlockSpec((1, dv), lambda b, c: (0, 0)),
        ],
        out_specs=[pl.BlockSpec((chunk, dv), lambda b, c: (b * n_chunks + c, 0)),
                   pl.BlockSpec(state_blk, lambda b, c: (b, 0, 0, 0))],
        out_shape=[jax.ShapeDtypeStruct((rows, dv), BF16),
                   jax.ShapeDtypeStruct((bsz,) + state_blk[1:], F32)],
        scratch_shapes=[pltpu.VMEM(state_blk[1:], F32),
                        pltpu.VMEM((N_RET_HEADS, chunk, chunk), F32)],
        compiler_params=_params(est),
        name="retention",
    )(q_src[0], k_src[0], v_src[0], sg_src[0], state, gn_g.reshape(1, dv))


def _mm_norm_kernel(*refs, n_k, emit_next, sub):
    if emit_next:
        a_ref, w_ref, res_ref, nw_ref, nw2_ref, o_ref, h_ref, acc_ref = refs
    else:
        a_ref, w_ref, res_ref, nw_ref, o_ref, acc_ref = refs
    kk = pl.program_id(1)
    tm = a_ref.shape[0]

    if n_k > 1:
        @pl.when(kk == 0)
        def _():
            acc_ref[...] = _dot(a_ref[...], w_ref[...])
    if n_k > 2:
        @pl.when((kk > 0) & (kk < n_k - 1))
        def _():
            acc_ref[...] += _dot(a_ref[...], w_ref[...])

    @pl.when(kk == n_k - 1)
    def _():
        for s in range(tm // sub):
            rs = slice(s * sub, (s + 1) * sub)
            m = _dot(a_ref[rs, :], w_ref[...])
            if n_k > 1:
                m = m + acc_ref[rs, :]
            inv = lax.rsqrt(jnp.mean(m * m, axis=-1, keepdims=True) + EPS)
            y = res_ref[rs, :] + m * inv * nw_ref[...]
            o_ref[rs, :] = y
            if emit_next:
                inv2 = lax.rsqrt(jnp.mean(y * y, axis=-1, keepdims=True) + EPS)
                h_ref[rs, :] = (y * inv2 * nw2_ref[...]).astype(h_ref.dtype)


def _matmul_norm_residual(a, w, res, norm_w, next_norm_w, tm, tk, name):
    rows, k = a.shape
    d = w.shape[1]
    n_k = k // tk
    emit_next = next_norm_w is not None
    row_i = lambda i, kk: (i, 0)
    const2 = lambda i, kk: (0, 0)
    in_specs = [pl.BlockSpec((tm, tk), lambda i, kk: (i, kk)),
                pl.BlockSpec((tk, d), lambda i, kk: (kk, 0)),
                pl.BlockSpec((tm, d), row_i),
                pl.BlockSpec((1, d), const2)]
    args = [a, w, res, norm_w.reshape(1, d)]
    out_specs = [pl.BlockSpec((tm, d), row_i)]
    out_shape = [jax.ShapeDtypeStruct((rows, d), F32)]
    acc_shape = (tm, d) if n_k > 1 else (SUBLANES, LANES)
    est = (2 * tm * tk * 2 + 2 * tk * d * 2 + 4 * tm * d * 4 + _nbytes(acc_shape, F32)
           + 4 * NORM_SUB_ROWS * d * 4)
    if emit_next:
        in_specs.append(pl.BlockSpec((1, d), const2))
        args.append(next_norm_w.reshape(1, d))
        out_specs.append(pl.BlockSpec((tm, d), row_i))
        out_shape.append(jax.ShapeDtypeStruct((rows, d), BF16))
        est += 2 * tm * d * 2
    outs = pl.pallas_call(
        functools.partial(_mm_norm_kernel, n_k=n_k, emit_next=emit_next, sub=min(tm, NORM_SUB_ROWS)),
        grid=(rows // tm, n_k),
        in_specs=in_specs,
        out_specs=out_specs,
        out_shape=out_shape,
        scratch_shapes=[pltpu.VMEM(acc_shape, F32)],
        compiler_params=_params(est),
        name=name,
    )(*args)
    return outs if emit_next else outs[0]


def _trunk_layer(x, conv_hist, ret_state, start, p):
    bsz, seq, d = x.shape
    rows = bsz * seq
    x2 = x.reshape(rows, d)
    d_qk = N_RET_HEADS * RET_DK
    d_v = N_RET_HEADS * RET_DV
    off_qk = 2 * d
    off_v = off_qk + 2 * d_qk
    off_sg = off_v + d_v
    off_gate = off_sg + d_v

    tm = min(rows, 1024)
    tn = 1024
    w_in = p["w_in"]

    h = _rms_norm_bf16(x2, p["norm_mix_pre"], min(rows, 512))
    n_tab = max(seq, tm)
    cos_t, sin_t = _rope_tables(start, seq, n_tab, min(n_tab, 1024))

    a = _matmul(h, [(w_in, 0), (w_in, d)], [], _epi_glu, d, F32, tm, 512, "in_proj_glu")
    if seq % tm == 0:
        c_pre, big = _in_proj_conv(h, w_in, a, conv_hist, p["conv_dw"], p["conv_dw_b"],
                                   cos_t, sin_t, p["b_gate"], seq, tm)
        q_src, k_src = (big, 0), (big, d_qk)
        v_src, sg_src = (big, 2 * d_qk), (big, 2 * d_qk + d_v)
        gate_c_src, gate_r_src = (big, 2 * d_qk + 2 * d_v), (big, 2 * d_qk + 2 * d_v + d)
    else:
        c_pre = _conv_module(a.reshape(bsz, seq, d), conv_hist, p["conv_dw"], p["conv_dw_b"],
                             min(seq, 128)).reshape(rows, d)
        tab_tiles = n_tab // tm
        tab_spec = pl.BlockSpec((tm, ROPE_HALF), lambda i, j: (i % tab_tiles, 0))
        qk = _matmul(h, [(w_in, off_qk)], [(cos_t, tab_spec), (sin_t, tab_spec)],
                     functools.partial(_epi_rotary, n_q_tiles=d_qk // tn),
                     2 * d_qk, BF16, tm, tn, "in_proj_qk")
        v = _matmul(h, [(w_in, off_v)], [], _epi_identity, d_v, BF16, tm, tn, "in_proj_v")
        sg = _matmul(h, [(w_in, off_sg)], [], _epi_silu, d_v, BF16, tm, tn, "in_proj_swish_gate")
        gates = _matmul(h, [(w_in, off_gate)], [(p["b_gate"].reshape(1, 2 * d), _row_spec(tn, 0))],
                        _epi_sigmoid_bias, 2 * d, BF16, tm, tn, "in_proj_merge_gates")
        q_src, k_src, v_src, sg_src = (qk, 0), (qk, d_qk), (v, 0), (sg, 0)
        gate_c_src, gate_r_src = (gates, 0), (gates, d)

    new_hist = a.reshape(bsz, seq, d)[:, seq - (CONV_WIDTH - 1):, :]
    chunk = min(seq, 256)
    o, s_new = _retention(q_src, k_src, v_src, sg_src, ret_state, p["ret_gn_g"], bsz, seq, chunk)

    tm_n = min(rows, 512)
    y_c = _conv_out_proj(c_pre, p["conv_ln_g"], p["conv_ln_b"], p["w_conv_out"], gate_c_src, tm_n, tn)
    mix_in = _matmul(o, [(p["w_ret_out"], 0)],
                     [(gate_r_src[0], _tile_spec(tm, tn, gate_r_src[1])), (y_c, _tile_spec(tm, tn, 0))],
                     _epi_gate_mul_add, d, BF16, tm, tn, "ret_out_proj")

    x1, h2 = _matmul_norm_residual(mix_in, p["w_out"], x2, p["norm_mix_post"], p["norm_ffn_pre"],
                                   tm_n, d, "out_proj_norm")

    d_ff = p["w_ffn_gate"].shape[1]
    f = _matmul(h2, [(p["w_ffn_gate"], 0), (p["w_ffn_up"], 0)], [], _epi_swiglu,
                d_ff, BF16, tm, 512, "ffn_up")
    y = _matmul_norm_residual(f, p["w_ffn_down"], x1, p["norm_ffn_post"], None, tm_n, d_ff // 2, "ffn_down_norm")
    return y.reshape(bsz, seq, d), new_hist, s_new


def kernel(x_prompt, x_sample, cache_conv, state_ret, norm_mix_pre, norm_mix_post, w_in, b_gate, conv_dw, conv_dw_b, conv_ln_g, conv_ln_b, w_conv_out, ret_gn_g, w_ret_out, w_out, norm_ffn_pre, norm_ffn_post, w_ffn_gate, w_ffn_up, w_ffn_down):
    depth = w_in.shape[0]
    past_len = 1024
    y_prompt, y_sample = x_prompt, x_sample
    conv_p, ret_p, conv_s, ret_s = [], [], [], []
    n_batch = x_prompt.shape[0]
    for l in range(depth):
        p = {
            "norm_mix_pre": norm_mix_pre[l], "norm_mix_post": norm_mix_post[l],
            "w_in": w_in[l].astype(BF16), "b_gate": b_gate[l],
            "conv_dw": conv_dw[l], "conv_dw_b": conv_dw_b[l],
            "conv_ln_g": conv_ln_g[l], "conv_ln_b": conv_ln_b[l],
            "w_conv_out": w_conv_out[l].astype(BF16), "ret_gn_g": ret_gn_g[l],
            "w_ret_out": w_ret_out[l].astype(BF16), "w_out": w_out[l].astype(BF16),
            "norm_ffn_pre": norm_ffn_pre[l], "norm_ffn_post": norm_ffn_post[l],
            "w_ffn_gate": w_ffn_gate[l].astype(BF16), "w_ffn_up": w_ffn_up[l].astype(BF16),
            "w_ffn_down": w_ffn_down[l].astype(BF16),
        }
        zero_hist = jnp.zeros((n_batch, CONV_WIDTH - 1, x_prompt.shape[-1]), x_prompt.dtype)
        zero_state = jnp.zeros((n_batch, N_RET_HEADS, RET_DK, RET_DV), x_prompt.dtype)
        y_prompt, hp, sp = _trunk_layer(y_prompt, zero_hist, zero_state, 0, p)
        y_sample, hs, ss = _trunk_layer(y_sample, cache_conv[l], state_ret[l], past_len, p)
        conv_p.append(hp)
        ret_p.append(sp)
        conv_s.append(hs)
        ret_s.append(ss)
    return (y_prompt, y_sample, jnp.stack(conv_p), jnp.stack(ret_p), jnp.stack(conv_s), jnp.stack(ret_s))
```

```python
import functools
import math

import jax
import jax.numpy as jnp
from jax import lax
from jax.experimental import pallas as pl
from jax.experimental.pallas import tpu as pltpu

F32 = jnp.float32
BF16 = jnp.bfloat16

EPS = 1e-6
ROPE_BASE = 10000.0
CONV_WIDTH = 31
N_RET_HEADS = 8
RET_DK = 256
RET_DV = 512
ROPE_HALF = RET_DK // 2

MIB = 1024 * 1024
V7X_VMEM_REQUEST_CAP = 56 * MIB
SUBLANES = 8
LANES = 128
CONV_HALO_ROWS = 32
CONV_LEAD = CONV_HALO_ROWS - (CONV_WIDTH - 1)
CONV_ROW_BLK = 64
NORM_SUB_ROWS = 128
MM_SUB_ROWS = 256
IN_PROJ_TILE = 1024


def _params(est_bytes):
    limit = min(V7X_VMEM_REQUEST_CAP, max(32 * MIB, int(est_bytes * 1.25)))
    return pltpu.CompilerParams(vmem_limit_bytes=limit)


def _nbytes(shape, dtype):
    return math.prod(shape) * jnp.dtype(dtype).itemsize


def _sigmoid(x):
    return jax.nn.sigmoid(x)


def _silu(x):
    return x * jax.nn.sigmoid(x)


def _dot(a, b):
    return jnp.dot(a, b, preferred_element_type=F32)


def _rms_kernel(x_ref, w_ref, o_ref):
    x = x_ref[...]
    ms = jnp.mean(x * x, axis=-1, keepdims=True)
    o_ref[...] = (x * lax.rsqrt(ms + EPS) * w_ref[...]).astype(o_ref.dtype)


def _rms_norm_bf16(x, w, tm):
    rows, d = x.shape
    est = 2 * tm * d * 4 + 2 * tm * d * 2
    return pl.pallas_call(
        _rms_kernel,
        grid=(rows // tm,),
        in_specs=[pl.BlockSpec((tm, d), lambda i: (i, 0)),
                  pl.BlockSpec((1, d), lambda i: (0, 0))],
        out_specs=pl.BlockSpec((tm, d), lambda i: (i, 0)),
        out_shape=jax.ShapeDtypeStruct((rows, d), BF16),
        compiler_params=_params(est),
        name="rms_norm",
    )(x, w.reshape(1, d))


def _mm_kernel(*refs, n_w, epilogue, sub):
    a_ref = refs[0]
    w_refs = refs[1:1 + n_w]
    extra_refs = refs[1 + n_w:-1]
    o_ref = refs[-1]
    for s in range(a_ref.shape[0] // sub):
        rs = slice(s * sub, (s + 1) * sub)
        a = a_ref[rs, :]
        accs = [_dot(a, w_ref[...]) for w_ref in w_refs]
        o_ref[rs, :] = epilogue(accs, extra_refs, rs).astype(o_ref.dtype)


def _matmul(a, ws, extras, epilogue, n_out, out_dtype, tm, tn, name):
    rows, k = a.shape
    grid = (rows // tm, n_out // tn)
    in_specs = [pl.BlockSpec((tm, k), lambda i, j: (i, 0))]
    args = [a]
    est = 2 * tm * k * 2
    for w, off in ws:
        assert off % tn == 0
        in_specs.append(pl.BlockSpec((k, tn), functools.partial(lambda i, j, o: (0, o + j), o=off // tn)))
        args.append(w)
        est += 2 * k * tn * 2
    for arr, spec in extras:
        in_specs.append(spec)
        args.append(arr)
        est += 2 * _nbytes(spec.block_shape, arr.dtype)
    est += 2 * tm * tn * jnp.dtype(out_dtype).itemsize
    est += (len(ws) + 1) * tm * tn * 4
    return pl.pallas_call(
        functools.partial(_mm_kernel, n_w=len(ws), epilogue=epilogue, sub=min(tm, MM_SUB_ROWS)),
        grid=grid,
        in_specs=in_specs,
        out_specs=pl.BlockSpec((tm, tn), lambda i, j: (i, j)),
        out_shape=jax.ShapeDtypeStruct((rows, n_out), out_dtype),
        compiler_params=_params(est),
        name=name,
    )(*args)


def _tile_spec(tm, tn, off):
    assert off % tn == 0
    return pl.BlockSpec((tm, tn), functools.partial(lambda i, j, o: (i, o + j), o=off // tn))


def _row_spec(tn, off):
    return pl.BlockSpec((1, tn), functools.partial(lambda i, j, o: (0, o + j), o=off // tn))


def _rotary(acc, cos, sin, scale):
    outs = []
    for h in range(acc.shape[-1] // RET_DK):
        x1 = acc[:, h * RET_DK:h * RET_DK + ROPE_HALF]
        x2 = acc[:, h * RET_DK + ROPE_HALF:(h + 1) * RET_DK]
        outs.append((x1 * cos - x2 * sin) * scale)
        outs.append((x1 * sin + x2 * cos) * scale)
    return jnp.concatenate(outs, axis=-1)


def _epi_glu(accs, extras, rs):
    return accs[0] * _sigmoid(accs[1])


def _epi_identity(accs, extras, rs):
    return accs[0]


def _epi_silu(accs, extras, rs):
    return _silu(accs[0])


def _epi_sigmoid_bias(accs, extras, rs):
    return _sigmoid(accs[0] + extras[0][...])


def _epi_rotary(accs, extras, rs, *, n_q_tiles):
    scale = jnp.where(pl.program_id(1) >= n_q_tiles, RET_DK ** -0.5, 1.0).astype(F32)
    return _rotary(accs[0], extras[0][rs, :], extras[1][rs, :], scale)


def _epi_gate_mul_add(accs, extras, rs):
    return extras[0][rs, :].astype(F32) * accs[0] + extras[1][rs, :]


def _epi_swiglu(accs, extras, rs):
    return _silu(accs[0]) * accs[1]


def _rope_kernel(inv_ref, cos_ref, sin_ref, *, start, seq_len, tt):
    row = pl.program_id(0) * tt + lax.broadcasted_iota(jnp.int32, (tt, ROPE_HALF), 0)
    pos = (start + lax.rem(row, seq_len)).astype(F32)
    ang = pos * inv_ref[...]
    cos_ref[...] = jnp.cos(ang)
    sin_ref[...] = jnp.sin(ang)


def _rope_tables(start, seq_len, n_rows, tt):
    inv_freq = 1.0 / (ROPE_BASE ** (jnp.arange(ROPE_HALF, dtype=F32) / ROPE_HALF))
    shape = jax.ShapeDtypeStruct((n_rows, ROPE_HALF), F32)
    return pl.pallas_call(
        functools.partial(_rope_kernel, start=start, seq_len=seq_len, tt=tt),
        grid=(n_rows // tt,),
        in_specs=[pl.BlockSpec((1, ROPE_HALF), lambda i: (0, 0))],
        out_specs=[pl.BlockSpec((tt, ROPE_HALF), lambda i: (i, 0))] * 2,
        out_shape=[shape, shape],
        name="rope_tables",
    )(inv_freq.reshape(1, ROPE_HALF))


def _conv_taps(x, w_row, bias, n_rows):
    acc = None
    for r in range(SUBLANES):
        rows = n_rows if r == 0 else n_rows + SUBLANES
        g = None
        for p in range((CONV_WIDTH + CONV_LEAD + SUBLANES - 1) // SUBLANES):
            j = SUBLANES * p + r - CONV_LEAD
            if 0 <= j < CONV_WIDTH:
                term = w_row(j) * x[SUBLANES * p:SUBLANES * p + rows]
                g = term if g is None else g + term
        shifted = g[r:r + n_rows]
        acc = shifted if acc is None else acc + shifted
    return acc + bias


def _conv_kernel(cur_ref, prev_ref, hist_ref, w_ref, b_ref, o_ref, win_ref, *, tt, row_blk):
    t = pl.program_id(1)

    @pl.when(t == 0)
    def _():
        win_ref[CONV_LEAD:CONV_HALO_ROWS, :] = hist_ref[0]

    @pl.when(t > 0)
    def _():
        win_ref[0:CONV_HALO_ROWS, :] = prev_ref[0]

    win_ref[CONV_HALO_ROWS:CONV_HALO_ROWS + tt, :] = cur_ref[0]

    d = cur_ref.shape[-1]
    for cb in range(d // LANES):
        cs = slice(cb * LANES, (cb + 1) * LANES)
        for rb in range(tt // row_blk):
            x = win_ref[rb * row_blk:rb * row_blk + row_blk + CONV_HALO_ROWS, cs]
            o_ref[0, rb * row_blk:(rb + 1) * row_blk, cs] = _conv_taps(
                x, lambda j: w_ref[j:j + 1, cs], b_ref[:, cs], row_blk)


def _conv_module(a, hist, conv_dw, conv_dw_b, tt):
    bsz, seq, d = a.shape
    n_hist = CONV_WIDTH - 1
    halo_per_tile = tt // CONV_HALO_ROWS
    row_blk = min(tt, CONV_ROW_BLK)
    est = 2 * tt * d * 4 + 4 * CONV_HALO_ROWS * d * 4 + 2 * tt * d * 4 + (tt + CONV_HALO_ROWS) * d * 4
    const2 = lambda b, t: (0, 0)
    return pl.pallas_call(
        functools.partial(_conv_kernel, tt=tt, row_blk=row_blk),
        grid=(bsz, seq // tt),
        in_specs=[
            pl.BlockSpec((1, tt, d), lambda b, t: (b, t, 0)),
            pl.BlockSpec((1, CONV_HALO_ROWS, d),
                         lambda b, t: (b, jnp.maximum(t * halo_per_tile - 1, 0), 0)),
            pl.BlockSpec((1, n_hist, d), lambda b, t: (b, 0, 0)),
            pl.BlockSpec((CONV_WIDTH, d), const2),
            pl.BlockSpec((1, d), const2),
        ],
        out_specs=pl.BlockSpec((1, tt, d), lambda b, t: (b, t, 0)),
        out_shape=jax.ShapeDtypeStruct((bsz, seq, d), F32),
        scratch_shapes=[pltpu.VMEM((CONV_HALO_ROWS + tt, d), F32)],
        compiler_params=_params(est),
        name="conv_module",
    )(a, a, hist, conv_dw, conv_dw_b.reshape(1, d))


def _in_proj_conv_kernel(h_ref, w_ref, acur_ref, aprev_ref, hist_ref, cw_ref, cb_ref, cos_ref, sin_ref,
                         bg_ref, cpre_ref, big_ref, win_ref, *, tm, tiles_per_seq, bounds):
    i = pl.program_id(0)
    t = pl.program_id(1)
    k_start, v_start, sg_start, gate_start = bounds
    n_sub = tm // MM_SUB_ROWS
    conv_blks_per_sub = (tm // CONV_ROW_BLK) // n_sub

    @pl.when(i % tiles_per_seq == 0)
    def _():
        win_ref[CONV_LEAD:CONV_HALO_ROWS, :] = hist_ref[0]

    @pl.when(i % tiles_per_seq != 0)
    def _():
        win_ref[0:CONV_HALO_ROWS, :] = aprev_ref[...]

    win_ref[CONV_HALO_ROWS:CONV_HALO_ROWS + tm, :] = acur_ref[...]

    def step(epilogue):
        half = w_ref.shape[-1] // 2
        conv_blk = 0
        for s in range(n_sub):
            rs = slice(s * MM_SUB_ROWS, (s + 1) * MM_SUB_ROWS)
            for c in range(2):
                cs = slice(c * half, (c + 1) * half)
                acc = _dot(h_ref[rs, :], w_ref[:, cs])
                for _ in range(conv_blks_per_sub // 2):
                    r0 = conv_blk * CONV_ROW_BLK
                    x = win_ref[r0:r0 + CONV_ROW_BLK + CONV_HALO_ROWS, :]
                    cpre_ref[r0:r0 + CONV_ROW_BLK, :] = _conv_taps(
                        x, lambda j: cw_ref[0, j:j + 1, :], cb_ref[0], CONV_ROW_BLK)
                    conv_blk += 1
                big_ref[rs, cs] = epilogue(acc, rs, cs).astype(big_ref.dtype)

    def epi_rotary(acc, rs, cs):
        scale = jnp.where(t >= k_start, RET_DK ** -0.5, 1.0).astype(F32)
        return _rotary(acc, cos_ref[rs, :], sin_ref[rs, :], scale)

    pl.when(t < v_start)(lambda: step(epi_rotary))
    pl.when((t >= v_start) & (t < sg_start))(lambda: step(lambda acc, rs, cs: acc))
    pl.when((t >= sg_start) & (t < gate_start))(lambda: step(lambda acc, rs, cs: _silu(acc)))
    pl.when(t >= gate_start)(lambda: step(lambda acc, rs, cs: _sigmoid(acc + bg_ref[:, cs])))


def _in_proj_conv(h, w_in, a, hist, conv_dw, conv_dw_b, cos_t, sin_t, b_gate, seq, tm):
    rows, d = h.shape
    tn = IN_PROJ_TILE
    n_chunks = d // LANES
    d_qk = N_RET_HEADS * RET_DK
    d_v = N_RET_HEADS * RET_DV
    off_wide = 2 * d
    wide_cols = 2 * d_qk + 2 * d_v + 2 * d
    assert wide_cols // tn == n_chunks and seq % tm == 0 and tm % CONV_HALO_ROWS == 0
    tiles_per_seq = seq // tm
    tab_tiles = cos_t.shape[0] // tm
    halo_per_tile = tm // CONV_HALO_ROWS
    bounds = (d_qk // tn, 2 * d_qk // tn, (2 * d_qk + d_v) // tn, (2 * d_qk + 2 * d_v) // tn)
    n_bias_steps = 2 * d // tn
    chunk_map = lambda i, t: (t, 0, 0)
    tab_map = lambda i, t: (i % tab_tiles, 0)
    cw = conv_dw.reshape(CONV_WIDTH, n_chunks, LANES).transpose(1, 0, 2)
    cb = conv_dw_b.reshape(n_chunks, 1, LANES)
    est = (2 * tm * d * 2 + 2 * d * tn * 2 + 4 * tm * LANES * 4 + 4 * tm * ROPE_HALF * 4
           + 2 * tm * tn * 2 + (tm + CONV_HALO_ROWS) * LANES * 4 + 3 * MM_SUB_ROWS * tn * 4)
    return pl.pallas_call(
        functools.partial(_in_proj_conv_kernel, tm=tm, tiles_per_seq=tiles_per_seq, bounds=bounds),
        grid=(rows // tm, n_chunks),
        in_specs=[
            pl.BlockSpec((tm, d), lambda i, t: (i, 0)),
            pl.BlockSpec((d, tn), lambda i, t: (0, off_wide // tn + t)),
            pl.BlockSpec((tm, LANES), lambda i, t: (i, t)),
            pl.BlockSpec((CONV_HALO_ROWS, LANES), lambda i, t: (jnp.maximum(i * halo_per_tile - 1, 0), t)),
            pl.BlockSpec((1, CONV_WIDTH - 1, LANES), lambda i, t: (i // tiles_per_seq, 0, t)),
            pl.BlockSpec((1, CONV_WIDTH, LANES), chunk_map),
            pl.BlockSpec((1, 1, LANES), chunk_map),
            pl.BlockSpec((tm, ROPE_HALF), tab_map),
            pl.BlockSpec((tm, ROPE_HALF), tab_map),
            pl.BlockSpec((1, tn), lambda i, t: (0, jnp.clip(t - bounds[3], 0, n_bias_steps - 1))),
        ],
        out_specs=[
            pl.BlockSpec((tm, LANES), lambda i, t: (i, t)),
            pl.BlockSpec((tm, tn), lambda i, t: (i, t)),
        ],
        out_shape=[jax.ShapeDtypeStruct((rows, d), F32),
                   jax.ShapeDtypeStruct((rows, wide_cols), BF16)],
        scratch_shapes=[pltpu.VMEM((CONV_HALO_ROWS + tm, LANES), F32)],
        compiler_params=_params(est),
        name="in_proj_conv",
    )(h, w_in, a, a, hist, cw, cb, cos_t, sin_t, b_gate.reshape(1, 2 * d))


def _ln_mm_kernel(c_ref, g_ref, b_ref, w_ref, gate_ref, o_ref, c_scr, *, sub):
    j = pl.program_id(1)
    tm = c_ref.shape[0]

    @pl.when(j == 0)
    def _():
        for s in range(tm // sub):
            rs = slice(s * sub, (s + 1) * sub)
            c = c_ref[rs, :]
            mu = jnp.mean(c, axis=-1, keepdims=True)
            cc = c - mu
            var = jnp.mean(cc * cc, axis=-1, keepdims=True)
            y = cc * lax.rsqrt(var + EPS) * g_ref[...] + b_ref[...]
            act = _silu(y).astype(c_scr.dtype)
            c_scr[rs, :] = act
            o_ref[rs, :] = gate_ref[rs, :].astype(F32) * _dot(act, w_ref[...])

    @pl.when(j > 0)
    def _():
        o_ref[...] = gate_ref[...].astype(F32) * _dot(c_scr[...], w_ref[...])


def _conv_out_proj(c_pre, ln_g, ln_b, w, gate_src, tm, tn):
    rows, d = c_pre.shape
    n_out = w.shape[1]
    gates, gate_off = gate_src
    est = 2 * tm * d * 4 + 2 * d * tn * 2 + 2 * tm * tn * 2 + 2 * tm * tn * 4 + tm * d * 2 + 6 * NORM_SUB_ROWS * d * 4
    const2 = lambda i, j: (0, 0)
    return pl.pallas_call(
        functools.partial(_ln_mm_kernel, sub=min(tm, NORM_SUB_ROWS)),
        grid=(rows // tm, n_out // tn),
        in_specs=[pl.BlockSpec((tm, d), lambda i, j: (i, 0)),
                  pl.BlockSpec((1, d), const2),
                  pl.BlockSpec((1, d), const2),
                  pl.BlockSpec((d, tn), lambda i, j: (0, j)),
                  _tile_spec(tm, tn, gate_off)],
        out_specs=pl.BlockSpec((tm, tn), lambda i, j: (i, j)),
        out_shape=jax.ShapeDtypeStruct((rows, n_out), F32),
        scratch_shapes=[pltpu.VMEM((tm, d), BF16)],
        compiler_params=_params(est),
        name="conv_out_proj",
    )(c_pre, ln_g.reshape(1, d), ln_b.reshape(1, d), w, gates)


def _log_decay(h):
    return math.log1p(-(2.0 ** (-5.0 - h)))


def _retention_kernel(q_ref, k_ref, v_ref, sg_ref, s0_ref, gn_ref, o_ref, s_out_ref,
                      s_ref, decay_ref, *, chunk, n_chunks):
    b = pl.program_id(0)
    c = pl.program_id(1)

    @pl.when((b == 0) & (c == 0))
    def _():
        n = lax.broadcasted_iota(jnp.int32, (chunk, chunk), 0)
        m = lax.broadcasted_iota(jnp.int32, (chunk, chunk), 1)
        diff = (n - m).astype(F32)
        for h in range(N_RET_HEADS):
            decay_ref[h] = jnp.where(diff >= 0.0, jnp.exp(jnp.maximum(diff, 0.0) * _log_decay(h)), 0.0)

    @pl.when(c == 0)
    def _():
        s_ref[...] = s0_ref[0]

    idx = lax.broadcasted_iota(jnp.int32, (chunk, 1), 0).astype(F32)
    for h in range(N_RET_HEADS):
        lg = _log_decay(h)
        q = q_ref[:, h * RET_DK:(h + 1) * RET_DK]
        k = k_ref[:, h * RET_DK:(h + 1) * RET_DK]
        v = v_ref[:, h * RET_DV:(h + 1) * RET_DV]
        s_old = s_ref[h]
        scores = lax.dot_general(q, k, (((1,), (1,)), ((), ())), preferred_element_type=F32)
        scores = (scores * decay_ref[h]).astype(BF16)
        o = _dot(scores, v)
        xi = jnp.exp((idx + 1.0) * lg)
        o = o + _dot(q, s_old.astype(BF16)) * xi
        zeta = jnp.exp((chunk - 1.0 - idx) * lg)
        kz = (k.astype(F32) * zeta).astype(BF16)
        s_ref[h] = math.exp(chunk * lg) * s_old + lax.dot_general(
            kz, v, (((0,), (0,)), ((), ())), preferred_element_type=F32)
        mu = jnp.mean(o, axis=-1, keepdims=True)
        oc = o - mu
        var = jnp.mean(oc * oc, axis=-1, keepdims=True)
        on = oc * lax.rsqrt(var + EPS) * gn_ref[:, h * RET_DV:(h + 1) * RET_DV]
        gate = sg_ref[:, h * RET_DV:(h + 1) * RET_DV].astype(F32)
        o_ref[:, h * RET_DV:(h + 1) * RET_DV] = (on * gate).astype(o_ref.dtype)

    @pl.when(c == n_chunks - 1)
    def _():
        s_out_ref[0] = s_ref[...]


def _retention(q_src, k_src, v_src, sg_src, state, gn_g, bsz, seq, chunk):
    rows = bsz * seq
    n_chunks = seq // chunk
    dqk = N_RET_HEADS * RET_DK
    dv = N_RET_HEADS * RET_DV
    state_blk = (1, N_RET_HEADS, RET_DK, RET_DV)
    est = (2 * 2 * chunk * dqk * 2 + 3 * 2 * chunk * dv * 2 + 4 * _nbytes(state_blk, F32)
           + _nbytes(state_blk, F32) + N_RET_HEADS * chunk * chunk * 4 + 8 * chunk * RET_DV * 4)

    def src_spec(src, width):
        assert src[1] % width == 0
        return pl.BlockSpec((chunk, width), functools.partial(
            lambda b, c, o: (b * n_chunks + c, o), o=src[1] // width))

    return pl.pallas_call(
        functools.partial(_retention_kernel, chunk=chunk, n_chunks=n_chunks),
        grid=(bsz, n_chunks),
        in_specs=[
            src_spec(q_src, dqk), src_spec(k_src, dqk), src_spec(v_src, dv), src_spec(sg_src, dv),
            pl.BlockSpec(state_blk, lambda b, c: (b, 0, 0, 0)),
            pl.BlockSpec((1, dv), lambda b, c: (0, 0)),
        ],
        out_specs=[pl.BlockSpec((chunk, dv), lambda b, c: (b * n_chunks + c, 0)),
                   pl.BlockSpec(state_blk, lambda b, c: (b, 0, 0, 0))],
        out_shape=[jax.ShapeDtypeStruct((rows, dv), BF16),
                   jax.ShapeDtypeStruct((bsz,) + state_blk[1:], F32)],
        scratch_shapes=[pltpu.VMEM(state_blk[1:], F32),
                        pltpu.VMEM((N_RET_HEADS, chunk, chunk), F32)],
        compiler_params=_params(est),
        name="retention",
    )(q_src[0], k_src[0], v_src[0], sg_src[0], state, gn_g.reshape(1, dv))


def _mm_norm_kernel(*refs, n_k, emit_next, sub):
    if emit_next:
        a_ref, w_ref, res_ref, nw_ref, nw2_ref, o_ref, h_ref, acc_ref = refs
    else:
        a_ref, w_ref, res_ref, nw_ref, o_ref, acc_ref = refs
    kk = pl.program_id(1)
    tm = a_ref.shape[0]

    if n_k > 1:
        @pl.when(kk == 0)
        def _():
            acc_ref[...] = _dot(a_ref[...], w_ref[...])
    if n_k > 2:
        @pl.when((kk > 0) & (kk < n_k - 1))
        def _():
            acc_ref[...] += _dot(a_ref[...], w_ref[...])

    @pl.when(kk == n_k - 1)
    def _():
        for s in range(tm // sub):
            rs = slice(s * sub, (s + 1) * sub)
            m = _dot(a_ref[rs, :], w_ref[...])
            if n_k > 1:
                m = m + acc_ref[rs, :]
            inv = lax.rsqrt(jnp.mean(m * m, axis=-1, keepdims=True) + EPS)
            y = res_ref[rs, :] + m * inv * nw_ref[...]
            o_ref[rs, :] = y
            if emit_next:
                inv2 = lax.rsqrt(jnp.mean(y * y, axis=-1, keepdims=True) + EPS)
                h_ref[rs, :] = (y * inv2 * nw2_ref[...]).astype(h_ref.dtype)


def _matmul_norm_residual(a, w, res, norm_w, next_norm_w, tm, tk, name):
    rows, k = a.shape
    d = w.shape[1]
    n_k = k // tk
    emit_next = next_norm_w is not None
    row_i = lambda i, kk: (i, 0)
    const2 = lambda i, kk: (0, 0)
    in_specs = [pl.BlockSpec((tm, tk), lambda i, kk: (i, kk)),
                pl.BlockSpec((tk, d), lambda i, kk: (kk, 0)),
                pl.BlockSpec((tm, d), row_i),
                pl.BlockSpec((1, d), const2)]
    args = [a, w, res, norm_w.reshape(1, d)]
    out_specs = [pl.BlockSpec((tm, d), row_i)]
    out_shape = [jax.ShapeDtypeStruct((rows, d), F32)]
    acc_shape = (tm, d) if n_k > 1 else (SUBLANES, LANES)
    est = (2 * tm * tk * 2 + 2 * tk * d * 2 + 4 * tm * d * 4 + _nbytes(acc_shape, F32)
           + 4 * NORM_SUB_ROWS * d * 4)
    if emit_next:
        in_specs.append(pl.BlockSpec((1, d), const2))
        args.append(next_norm_w.reshape(1, d))
        out_specs.append(pl.BlockSpec((tm, d), row_i))
        out_shape.append(jax.ShapeDtypeStruct((rows, d), BF16))
        est += 2 * tm * d * 2
    outs = pl.pallas_call(
        functools.partial(_mm_norm_kernel, n_k=n_k, emit_next=emit_next, sub=min(tm, NORM_SUB_ROWS)),
        grid=(rows // tm, n_k),
        in_specs=in_specs,
        out_specs=out_specs,
        out_shape=out_shape,
        scratch_shapes=[pltpu.VMEM(acc_shape, F32)],
        compiler_params=_params(est),
        name=name,
    )(*args)
    return outs if emit_next else outs[0]


def _trunk_layer(x, conv_hist, ret_state, start, p):
    bsz, seq, d = x.shape
    rows = bsz * seq
    x2 = x.reshape(rows, d)
    d_qk = N_RET_HEADS * RET_DK
    d_v = N_RET_HEADS * RET_DV
    off_qk = 2 * d
    off_v = off_qk + 2 * d_qk
    off_sg = off_v + d_v
    off_gate = off_sg + d_v

    tm = min(rows, 1024)
    tm_big = 2 * tm if rows % (2 * tm) == 0 else tm
    tn = 1024
    w_in = p["w_in"]

    h = _rms_norm_bf16(x2, p["norm_mix_pre"], min(rows, 512))
    n_tab = max(seq, tm_big)
    cos_t, sin_t = _rope_tables(start, seq, n_tab, min(n_tab, 1024))

    a = _matmul(h, [(w_in, 0), (w_in, d)], [], _epi_glu, d, F32, tm_big, 512, "in_proj_glu")
    if seq % tm_big == 0:
        c_pre, big = _in_proj_conv(h, w_in, a, conv_hist, p["conv_dw"], p["conv_dw_b"],
                                   cos_t, sin_t, p["b_gate"], seq, tm_big)
        q_src, k_src = (big, 0), (big, d_qk)
        v_src, sg_src = (big, 2 * d_qk), (big, 2 * d_qk + d_v)
        gate_c_src, gate_r_src = (big, 2 * d_qk + 2 * d_v), (big, 2 * d_qk + 2 * d_v + d)
    else:
        c_pre = _conv_module(a.reshape(bsz, seq, d), conv_hist, p["conv_dw"], p["conv_dw_b"],
                             min(seq, 128)).reshape(rows, d)
        tab_tiles = n_tab // tm
        tab_spec = pl.BlockSpec((tm, ROPE_HALF), lambda i, j: (i % tab_tiles, 0))
        qk = _matmul(h, [(w_in, off_qk)], [(cos_t, tab_spec), (sin_t, tab_spec)],
                     functools.partial(_epi_rotary, n_q_tiles=d_qk // tn),
                     2 * d_qk, BF16, tm, tn, "in_proj_qk")
        v = _matmul(h, [(w_in, off_v)], [], _epi_identity, d_v, BF16, tm, tn, "in_proj_v")
        sg = _matmul(h, [(w_in, off_sg)], [], _epi_silu, d_v, BF16, tm, tn, "in_proj_swish_gate")
        gates = _matmul(h, [(w_in, off_gate)], [(p["b_gate"].reshape(1, 2 * d), _row_spec(tn, 0))],
                        _epi_sigmoid_bias, 2 * d, BF16, tm, tn, "in_proj_merge_gates")
        q_src, k_src, v_src, sg_src = (qk, 0), (qk, d_qk), (v, 0), (sg, 0)
        gate_c_src, gate_r_src = (gates, 0), (gates, d)

    new_hist = a.reshape(bsz, seq, d)[:, seq - (CONV_WIDTH - 1):, :]
    chunk = min(seq, 256)
    o, s_new = _retention(q_src, k_src, v_src, sg_src, ret_state, p["ret_gn_g"], bsz, seq, chunk)

    tm_n = min(rows, 512)
    y_c = _conv_out_proj(c_pre, p["conv_ln_g"], p["conv_ln_b"], p["w_conv_out"], gate_c_src, tm, tn)
    mix_in = _matmul(o, [(p["w_ret_out"], 0)],
                     [(gate_r_src[0], _tile_spec(tm, tn, gate_r_src[1])), (y_c, _tile_spec(tm, tn, 0))],
                     _epi_gate_mul_add, d, BF16, tm, tn, "ret_out_proj")

    x1, h2 = _matmul_norm_residual(mix_in, p["w_out"], x2, p["norm_mix_post"], p["norm_ffn_pre"],
                                   tm_n, d, "out_proj_norm")

    d_ff = p["w_ffn_gate"].shape[1]
    f = _matmul(h2, [(p["w_ffn_gate"], 0), (p["w_ffn_up"], 0)], [], _epi_swiglu,
                d_ff, BF16, tm_big, 512, "ffn_up")
    y = _matmul_norm_residual(f, p["w_ffn_down"], x1, p["norm_ffn_post"], None, tm_n, d_ff // 2, "ffn_down_norm")
    return y.reshape(bsz, seq, d), new_hist, s_new


def kernel(x_prompt, x_sample, cache_conv, state_ret, norm_mix_pre, norm_mix_post, w_in, b_gate, conv_dw, conv_dw_b, conv_ln_g, conv_ln_b, w_conv_out, ret_gn_g, w_ret_out, w_out, norm_ffn_pre, norm_ffn_post, w_ffn_gate, w_ffn_up, w_ffn_down):
    depth = w_in.shape[0]
    past_len = 1024
    y_prompt, y_sample = x_prompt, x_sample
    conv_p, ret_p, conv_s, ret_s = [], [], [], []
    n_batch = x_prompt.shape[0]
    for l in range(depth):
        p = {
            "norm_mix_pre": norm_mix_pre[l], "norm_mix_post": norm_mix_post[l],
            "w_in": w_in[l].astype(BF16), "b_gate": b_gate[l],
            "conv_dw": conv_dw[l], "conv_dw_b": conv_dw_b[l],
            "conv_ln_g": conv_ln_g[l], "conv_ln_b": conv_ln_b[l],
            "w_conv_out": w_conv_out[l].astype(BF16), "ret_gn_g": ret_gn_g[l],
            "w_ret_out": w_ret_out[l].astype(BF16), "w_out": w_out[l].astype(BF16),
            "norm_ffn_pre": norm_ffn_pre[l], "norm_ffn_post": norm_ffn_post[l],
            "w_ffn_gate": w_ffn_gate[l].astype(BF16), "w_ffn_up": w_ffn_up[l].astype(BF16),
            "w_ffn_down": w_ffn_down[l].astype(BF16),
        }
        zero_hist = jnp.zeros((n_batch, CONV_WIDTH - 1, x_prompt.shape[-1]), x_prompt.dtype)
        zero_state = jnp.zeros((n_batch, N_RET_HEADS, RET_DK, RET_DV), x_prompt.dtype)
        y_prompt, hp, sp = _trunk_layer(y_prompt, zero_hist, zero_state, 0, p)
        y_sample, hs, ss = _trunk_layer(y_sample, cache_conv[l], state_ret[l], past_len, p)
        conv_p.append(hp)
        ret_p.append(sp)
        conv_s.append(hs)
        ret_s.append(ss)
    return (y_prompt, y_sample, jnp.stack(conv_p), jnp.stack(ret_p), jnp.stack(conv_s), jnp.stack(ret_s))
```

```python
import functools
import math

import jax
import jax.numpy as jnp
from jax import lax
from jax.experimental import pallas as pl
from jax.experimental.pallas import tpu as pltpu

F32 = jnp.float32
BF16 = jnp.bfloat16

EPS = 1e-6
ROPE_BASE = 10000.0
CONV_WIDTH = 31
N_RET_HEADS = 8
RET_DK = 256
RET_DV = 512
ROPE_HALF = RET_DK // 2

MIB = 1024 * 1024
V7X_VMEM_REQUEST_CAP = 56 * MIB
SUBLANES = 8
LANES = 128
CONV_HALO_ROWS = 32
CONV_LEAD = CONV_HALO_ROWS - (CONV_WIDTH - 1)
CONV_ROW_BLK = 64
NORM_SUB_ROWS = 128
MM_SUB_ROWS = 256
IN_PROJ_UNROLL = 2
IN_PROJ_TILE = 1024


def _params(est_bytes):
    limit = min(V7X_VMEM_REQUEST_CAP, max(32 * MIB, int(est_bytes * 1.25)))
    return pltpu.CompilerParams(vmem_limit_bytes=limit)


def _nbytes(shape, dtype):
    return math.prod(shape) * jnp.dtype(dtype).itemsize


def _sigmoid(x):
    return jax.nn.sigmoid(x)


def _silu(x):
    return x * jax.nn.sigmoid(x)


def _dot(a, b):
    return jnp.dot(a, b, preferred_element_type=F32)


def _rms_kernel(x_ref, w_ref, o_ref):
    x = x_ref[...]
    ms = jnp.mean(x * x, axis=-1, keepdims=True)
    o_ref[...] = (x * lax.rsqrt(ms + EPS) * w_ref[...]).astype(o_ref.dtype)


def _rms_norm_bf16(x, w, tm):
    rows, d = x.shape
    est = 2 * tm * d * 4 + 2 * tm * d * 2
    return pl.pallas_call(
        _rms_kernel,
        grid=(rows // tm,),
        in_specs=[pl.BlockSpec((tm, d), lambda i: (i, 0)),
                  pl.BlockSpec((1, d), lambda i: (0, 0))],
        out_specs=pl.BlockSpec((tm, d), lambda i: (i, 0)),
        out_shape=jax.ShapeDtypeStruct((rows, d), BF16),
        compiler_params=_params(est),
        name="rms_norm",
    )(x, w.reshape(1, d))


def _mm_kernel(*refs, n_w, epilogue, sub):
    a_ref = refs[0]
    w_refs = refs[1:1 + n_w]
    extra_refs = refs[1 + n_w:-1]
    o_ref = refs[-1]
    for s in range(a_ref.shape[0] // sub):
        rs = slice(s * sub, (s + 1) * sub)
        a = a_ref[rs, :]
        accs = [_dot(a, w_ref[...]) for w_ref in w_refs]
        o_ref[rs, :] = epilogue(accs, extra_refs, rs).astype(o_ref.dtype)


def _matmul(a, ws, extras, epilogue, n_out, out_dtype, tm, tn, name):
    rows, k = a.shape
    grid = (rows // tm, n_out // tn)
    in_specs = [pl.BlockSpec((tm, k), lambda i, j: (i, 0))]
    args = [a]
    est = 2 * tm * k * 2
    for w, off in ws:
        assert off % tn == 0
        in_specs.append(pl.BlockSpec((k, tn), functools.partial(lambda i, j, o: (0, o + j), o=off // tn)))
        args.append(w)
        est += 2 * k * tn * 2
    for arr, spec in extras:
        in_specs.append(spec)
        args.append(arr)
        est += 2 * _nbytes(spec.block_shape, arr.dtype)
    est += 2 * tm * tn * jnp.dtype(out_dtype).itemsize
    est += (len(ws) + 1) * tm * tn * 4
    return pl.pallas_call(
        functools.partial(_mm_kernel, n_w=len(ws), epilogue=epilogue, sub=min(tm, MM_SUB_ROWS)),
        grid=grid,
        in_specs=in_specs,
        out_specs=pl.BlockSpec((tm, tn), lambda i, j: (i, j)),
        out_shape=jax.ShapeDtypeStruct((rows, n_out), out_dtype),
        compiler_params=_params(est),
        name=name,
    )(*args)


def _tile_spec(tm, tn, off):
    assert off % tn == 0
    return pl.BlockSpec((tm, tn), functools.partial(lambda i, j, o: (i, o + j), o=off // tn))


def _row_spec(tn, off):
    return pl.BlockSpec((1, tn), functools.partial(lambda i, j, o: (0, o + j), o=off // tn))


def _rotary(acc, cos, sin, scale):
    outs = []
    for h in range(acc.shape[-1] // RET_DK):
        x1 = acc[:, h * RET_DK:h * RET_DK + ROPE_HALF]
        x2 = acc[:, h * RET_DK + ROPE_HALF:(h + 1) * RET_DK]
        outs.append((x1 * cos - x2 * sin) * scale)
        outs.append((x1 * sin + x2 * cos) * scale)
    return jnp.concatenate(outs, axis=-1)


def _epi_glu(accs, extras, rs):
    return accs[0] * _sigmoid(accs[1])


def _epi_identity(accs, extras, rs):
    return accs[0]


def _epi_silu(accs, extras, rs):
    return _silu(accs[0])


def _epi_sigmoid_bias(accs, extras, rs):
    return _sigmoid(accs[0] + extras[0][...])


def _epi_rotary(accs, extras, rs, *, n_q_tiles):
    scale = jnp.where(pl.program_id(1) >= n_q_tiles, RET_DK ** -0.5, 1.0).astype(F32)
    return _rotary(accs[0], extras[0][rs, :], extras[1][rs, :], scale)


def _epi_gate_mul_add(accs, extras, rs):
    return extras[0][rs, :].astype(F32) * accs[0] + extras[1][rs, :]


def _epi_swiglu(accs, extras, rs):
    return _silu(accs[0]) * accs[1]


def _rope_kernel(inv_ref, cos_ref, sin_ref, *, start, seq_len, tt):
    row = pl.program_id(0) * tt + lax.broadcasted_iota(jnp.int32, (tt, ROPE_HALF), 0)
    pos = (start + lax.rem(row, seq_len)).astype(F32)
    ang = pos * inv_ref[...]
    cos_ref[...] = jnp.cos(ang)
    sin_ref[...] = jnp.sin(ang)


def _rope_tables(start, seq_len, n_rows, tt):
    inv_freq = 1.0 / (ROPE_BASE ** (jnp.arange(ROPE_HALF, dtype=F32) / ROPE_HALF))
    shape = jax.ShapeDtypeStruct((n_rows, ROPE_HALF), F32)
    return pl.pallas_call(
        functools.partial(_rope_kernel, start=start, seq_len=seq_len, tt=tt),
        grid=(n_rows // tt,),
        in_specs=[pl.BlockSpec((1, ROPE_HALF), lambda i: (0, 0))],
        out_specs=[pl.BlockSpec((tt, ROPE_HALF), lambda i: (i, 0))] * 2,
        out_shape=[shape, shape],
        name="rope_tables",
    )(inv_freq.reshape(1, ROPE_HALF))


def _conv_taps(x, w_row, bias, n_rows):
    acc = None
    for r in range(SUBLANES):
        rows = n_rows if r == 0 else n_rows + SUBLANES
        g = None
        for p in range((CONV_WIDTH + CONV_LEAD + SUBLANES - 1) // SUBLANES):
            j = SUBLANES * p + r - CONV_LEAD
            if 0 <= j < CONV_WIDTH:
                term = w_row(j) * x[SUBLANES * p:SUBLANES * p + rows]
                g = term if g is None else g + term
        shifted = g[r:r + n_rows]
        acc = shifted if acc is None else acc + shifted
    return acc + bias


def _conv_kernel(cur_ref, prev_ref, hist_ref, w_ref, b_ref, o_ref, win_ref, *, tt, row_blk):
    t = pl.program_id(1)

    @pl.when(t == 0)
    def _():
        win_ref[CONV_LEAD:CONV_HALO_ROWS, :] = hist_ref[0]

    @pl.when(t > 0)
    def _():
        win_ref[0:CONV_HALO_ROWS, :] = prev_ref[0]

    win_ref[CONV_HALO_ROWS:CONV_HALO_ROWS + tt, :] = cur_ref[0]

    d = cur_ref.shape[-1]
    for cb in range(d // LANES):
        cs = slice(cb * LANES, (cb + 1) * LANES)
        for rb in range(tt // row_blk):
            x = win_ref[rb * row_blk:rb * row_blk + row_blk + CONV_HALO_ROWS, cs]
            o_ref[0, rb * row_blk:(rb + 1) * row_blk, cs] = _conv_taps(
                x, lambda j: w_ref[j:j + 1, cs], b_ref[:, cs], row_blk)


def _conv_module(a, hist, conv_dw, conv_dw_b, tt):
    bsz, seq, d = a.shape
    n_hist = CONV_WIDTH - 1
    halo_per_tile = tt // CONV_HALO_ROWS
    row_blk = min(tt, CONV_ROW_BLK)
    est = 2 * tt * d * 4 + 4 * CONV_HALO_ROWS * d * 4 + 2 * tt * d * 4 + (tt + CONV_HALO_ROWS) * d * 4
    const2 = lambda b, t: (0, 0)
    return pl.pallas_call(
        functools.partial(_conv_kernel, tt=tt, row_blk=row_blk),
        grid=(bsz, seq // tt),
        in_specs=[
            pl.BlockSpec((1, tt, d), lambda b, t: (b, t, 0)),
            pl.BlockSpec((1, CONV_HALO_ROWS, d),
                         lambda b, t: (b, jnp.maximum(t * halo_per_tile - 1, 0), 0)),
            pl.BlockSpec((1, n_hist, d), lambda b, t: (b, 0, 0)),
            pl.BlockSpec((CONV_WIDTH, d), const2),
            pl.BlockSpec((1, d), const2),
        ],
        out_specs=pl.BlockSpec((1, tt, d), lambda b, t: (b, t, 0)),
        out_shape=jax.ShapeDtypeStruct((bsz, seq, d), F32),
        scratch_shapes=[pltpu.VMEM((CONV_HALO_ROWS + tt, d), F32)],
        compiler_params=_params(est),
        name="conv_module",
    )(a, a, hist, conv_dw, conv_dw_b.reshape(1, d))


def _in_proj_conv_kernel(h_ref, w_ref, acur_ref, aprev_ref, hist_ref, cw_ref, cb_ref, cos_ref, sin_ref,
                         bg_ref, cpre_ref, big_ref, win_ref, *, tm, tiles_per_seq, bounds):
    i = pl.program_id(0)
    t = pl.program_id(1)
    k_start, v_start, sg_start, gate_start = bounds
    n_sub = tm // MM_SUB_ROWS
    conv_blks_per_sub = (tm // CONV_ROW_BLK) // n_sub

    @pl.when(i % tiles_per_seq == 0)
    def _():
        win_ref[CONV_LEAD:CONV_HALO_ROWS, :] = hist_ref[0, t]

    @pl.when(i % tiles_per_seq != 0)
    def _():
        win_ref[0:CONV_HALO_ROWS, :] = aprev_ref[...]

    win_ref[CONV_HALO_ROWS:CONV_HALO_ROWS + tm, :] = acur_ref[...]

    def step(epilogue):
        half = w_ref.shape[-1] // 2
        conv_blks_per_half = conv_blks_per_sub // 2

        def sub_tiles(it, carry):
            for u in range(IN_PROJ_UNROLL):
                r_sub = pl.multiple_of((it * IN_PROJ_UNROLL + u) * MM_SUB_ROWS, MM_SUB_ROWS)
                rs = pl.ds(r_sub, MM_SUB_ROWS)
                for c in range(2):
                    cs = slice(c * half, (c + 1) * half)
                    acc = _dot(h_ref[rs, :], w_ref[:, cs])
                    for b in range(conv_blks_per_half):
                        r0 = pl.multiple_of(r_sub + (c * conv_blks_per_half + b) * CONV_ROW_BLK, CONV_ROW_BLK)
                        x = win_ref[pl.ds(r0, CONV_ROW_BLK + CONV_HALO_ROWS), :]
                        cpre_ref[pl.ds(r0, CONV_ROW_BLK), :] = _conv_taps(
                            x, lambda j: cw_ref[t, j:j + 1, :], cb_ref[t], CONV_ROW_BLK)
                    big_ref[rs, cs] = epilogue(acc, rs, cs).astype(big_ref.dtype)
            return carry

        lax.fori_loop(0, n_sub // IN_PROJ_UNROLL, sub_tiles, 0)

    def epi_rotary(acc, rs, cs):
        scale = jnp.where(t >= k_start, RET_DK ** -0.5, 1.0).astype(F32)
        return _rotary(acc, cos_ref[rs, :], sin_ref[rs, :], scale)

    pl.when(t < v_start)(lambda: step(epi_rotary))
    pl.when((t >= v_start) & (t < sg_start))(lambda: step(lambda acc, rs, cs: acc))
    pl.when((t >= sg_start) & (t < gate_start))(lambda: step(lambda acc, rs, cs: _silu(acc)))
    pl.when(t >= gate_start)(lambda: step(lambda acc, rs, cs: _sigmoid(acc + bg_ref[t - gate_start, :, cs])))


def _in_proj_conv(h, w_in, a, hist, conv_dw, conv_dw_b, cos_t, sin_t, b_gate, seq, tm):
    rows, d = h.shape
    tn = IN_PROJ_TILE
    n_chunks = d // LANES
    d_qk = N_RET_HEADS * RET_DK
    d_v = N_RET_HEADS * RET_DV
    off_wide = 2 * d
    wide_cols = 2 * d_qk + 2 * d_v + 2 * d
    assert wide_cols // tn == n_chunks and seq % tm == 0 and tm % CONV_HALO_ROWS == 0
    tiles_per_seq = seq // tm
    tab_tiles = cos_t.shape[0] // tm
    halo_per_tile = tm // CONV_HALO_ROWS
    bounds = (d_qk // tn, 2 * d_qk // tn, (2 * d_qk + d_v) // tn, (2 * d_qk + 2 * d_v) // tn)
    n_bias_steps = 2 * d // tn
    whole3 = lambda i, t: (0, 0, 0)
    tab_map = lambda i, t: (i % tab_tiles, 0)
    cw = conv_dw.reshape(CONV_WIDTH, n_chunks, LANES).transpose(1, 0, 2)
    cb = conv_dw_b.reshape(n_chunks, 1, LANES)
    hist_cm = hist.reshape(hist.shape[0], CONV_WIDTH - 1, n_chunks, LANES).transpose(0, 2, 1, 3)
    est = (2 * tm * d * 2 + 2 * d * tn * 2 + 4 * tm * LANES * 4 + 4 * tm * ROPE_HALF * 4
           + 2 * tm * tn * 2 + (tm + CONV_HALO_ROWS) * LANES * 4 + 3 * MM_SUB_ROWS * tn * 4)
    return pl.pallas_call(
        functools.partial(_in_proj_conv_kernel, tm=tm, tiles_per_seq=tiles_per_seq, bounds=bounds),
        grid=(rows // tm, n_chunks),
        in_specs=[
            pl.BlockSpec((tm, d), lambda i, t: (i, 0)),
            pl.BlockSpec((d, tn), lambda i, t: (0, off_wide // tn + t)),
            pl.BlockSpec((tm, LANES), lambda i, t: (i, t)),
            pl.BlockSpec((CONV_HALO_ROWS, LANES), lambda i, t: (jnp.maximum(i * halo_per_tile - 1, 0), t)),
            pl.BlockSpec((1, n_chunks, CONV_WIDTH - 1, LANES), lambda i, t: (i // tiles_per_seq, 0, 0, 0)),
            pl.BlockSpec((n_chunks, CONV_WIDTH, LANES), whole3),
            pl.BlockSpec((n_chunks, 1, LANES), whole3),
            pl.BlockSpec((tm, ROPE_HALF), tab_map),
            pl.BlockSpec((tm, ROPE_HALF), tab_map),
            pl.BlockSpec((n_bias_steps, 1, tn), whole3),
        ],
        out_specs=[
            pl.BlockSpec((tm, LANES), lambda i, t: (i, t)),
            pl.BlockSpec((tm, tn), lambda i, t: (i, t)),
        ],
        out_shape=[jax.ShapeDtypeStruct((rows, d), F32),
                   jax.ShapeDtypeStruct((rows, wide_cols), BF16)],
        scratch_shapes=[pltpu.VMEM((CONV_HALO_ROWS + tm, LANES), F32)],
        compiler_params=_params(est),
        name="in_proj_conv",
    )(h, w_in, a, a, hist_cm, cw, cb, cos_t, sin_t, b_gate.reshape(n_bias_steps, 1, tn))


def _ln_mm_kernel(c_ref, g_ref, b_ref, w_ref, gate_ref, o_ref, c_scr, *, sub):
    j = pl.program_id(1)
    tm = c_ref.shape[0]

    @pl.when(j == 0)
    def _():
        for s in range(tm // sub):
            rs = slice(s * sub, (s + 1) * sub)
            c = c_ref[rs, :]
            mu = jnp.mean(c, axis=-1, keepdims=True)
            cc = c - mu
            var = jnp.mean(cc * cc, axis=-1, keepdims=True)
            y = cc * lax.rsqrt(var + EPS) * g_ref[...] + b_ref[...]
            act = _silu(y).astype(c_scr.dtype)
            c_scr[rs, :] = act
            o_ref[rs, :] = gate_ref[rs, :].astype(F32) * _dot(act, w_ref[...])

    @pl.when(j > 0)
    def _():
        o_ref[...] = gate_ref[...].astype(F32) * _dot(c_scr[...], w_ref[...])


def _conv_out_proj(c_pre, ln_g, ln_b, w, gate_src, tm, tn):
    rows, d = c_pre.shape
    n_out = w.shape[1]
    gates, gate_off = gate_src
    est = 2 * tm * d * 4 + 2 * d * tn * 2 + 2 * tm * tn * 2 + 2 * tm * tn * 4 + tm * d * 2 + 6 * NORM_SUB_ROWS * d * 4
    const2 = lambda i, j: (0, 0)
    return pl.pallas_call(
        functools.partial(_ln_mm_kernel, sub=min(tm, NORM_SUB_ROWS)),
        grid=(rows // tm, n_out // tn),
        in_specs=[pl.BlockSpec((tm, d), lambda i, j: (i, 0)),
                  pl.BlockSpec((1, d), const2),
                  pl.BlockSpec((1, d), const2),
                  pl.BlockSpec((d, tn), lambda i, j: (0, j)),
                  _tile_spec(tm, tn, gate_off)],
        out_specs=pl.BlockSpec((tm, tn), lambda i, j: (i, j)),
        out_shape=jax.ShapeDtypeStruct((rows, n_out), F32),
        scratch_shapes=[pltpu.VMEM((tm, d), BF16)],
        compiler_params=_params(est),
        name="conv_out_proj",
    )(c_pre, ln_g.reshape(1, d), ln_b.reshape(1, d), w, gates)


def _log_decay(h):
    return math.log1p(-(2.0 ** (-5.0 - h)))


def _retention_kernel(q_ref, k_ref, v_ref, sg_ref, s0_ref, gn_ref, o_ref, s_out_ref,
                      s_ref, decay_ref, *, chunk, n_chunks):
    b = pl.program_id(0)
    c = pl.program_id(1)

    @pl.when((b == 0) & (c == 0))
    def _():
        n = lax.broadcasted_iota(jnp.int32, (chunk, chunk), 0)
        m = lax.broadcasted_iota(jnp.int32, (chunk, chunk), 1)
        diff = (n - m).astype(F32)
        for h in range(N_RET_HEADS):
            decay_ref[h] = jnp.where(diff >= 0.0, jnp.exp(jnp.maximum(diff, 0.0) * _log_decay(h)), 0.0)

    @pl.when(c == 0)
    def _():
        s_ref[...] = s0_ref[0]

    idx = lax.broadcasted_iota(jnp.int32, (chunk, 1), 0).astype(F32)
    for h in range(N_RET_HEADS):
        lg = _log_decay(h)
        q = q_ref[:, h * RET_DK:(h + 1) * RET_DK]
        k = k_ref[:, h * RET_DK:(h + 1) * RET_DK]
        v = v_ref[:, h * RET_DV:(h + 1) * RET_DV]
        s_old = s_ref[h]
        scores = lax.dot_general(q, k, (((1,), (1,)), ((), ())), preferred_element_type=F32)
        scores = (scores * decay_ref[h]).astype(BF16)
        o = _dot(scores, v)
        xi = jnp.exp((idx + 1.0) * lg)
        o = o + _dot(q, s_old.astype(BF16)) * xi
        zeta = jnp.exp((chunk - 1.0 - idx) * lg)
        kz = (k.astype(F32) * zeta).astype(BF16)
        s_ref[h] = math.exp(chunk * lg) * s_old + lax.dot_general(
            kz, v, (((0,), (0,)), ((), ())), preferred_element_type=F32)
        mu = jnp.mean(o, axis=-1, keepdims=True)
        oc = o - mu
        var = jnp.mean(oc * oc, axis=-1, keepdims=True)
        on = oc * lax.rsqrt(var + EPS) * gn_ref[:, h * RET_DV:(h + 1) * RET_DV]
        gate = sg_ref[:, h * RET_DV:(h + 1) * RET_DV].astype(F32)
        o_ref[:, h * RET_DV:(h + 1) * RET_DV] = (on * gate).astype(o_ref.dtype)

    @pl.when(c == n_chunks - 1)
    def _():
        s_out_ref[0] = s_ref[...]


def _retention(q_src, k_src, v_src, sg_src, state, gn_g, bsz, seq, chunk):
    rows = bsz * seq
    n_chunks = seq // chunk
    dqk = N_RET_HEADS * RET_DK
    dv = N_RET_HEADS * RET_DV
    state_blk = (1, N_RET_HEADS, RET_DK, RET_DV)
    est = (2 * 2 * chunk * dqk * 2 + 3 * 2 * chunk * dv * 2 + 4 * _nbytes(state_blk, F32)
           + _nbytes(state_blk, F32) + N_RET_HEADS * chunk * chunk * 4 + 8 * chunk * RET_DV * 4)

    def src_spec(src, width):
        assert src[1] % width == 0
        return pl.BlockSpec((chunk, width), functools.partial(
            lambda b, c, o: (b * n_chunks + c, o), o=src[1] // width))

    return pl.pallas_call(
        functools.partial(_retention_kernel, chunk=chunk, n_chunks=n_chunks),
        grid=(bsz, n_chunks),
        in_specs=[
            src_spec(q_src, dqk), src_spec(k_src, dqk), src_spec(v_src, dv), src_spec(sg_src, dv),
            pl.BlockSpec(state_blk, lambda b, c: (b, 0, 0, 0)),
            pl.BlockSpec((1, dv), lambda b, c: (0, 0)),
        ],
        out_specs=[pl.BlockSpec((chunk, dv), lambda b, c: (b * n_chunks + c, 0)),
                   pl.BlockSpec(state_blk, lambda b, c: (b, 0, 0, 0))],
        out_shape=[jax.ShapeDtypeStruct((rows, dv), BF16),
                   jax.ShapeDtypeStruct((bsz,) + state_blk[1:], F32)],
        scratch_shapes=[pltpu.VMEM(state_blk[1:], F32),
                        pltpu.VMEM((N_RET_HEADS, chunk, chunk), F32)],
        compiler_params=_params(est),
        name="retention",
    )(q_src[0], k_src[0], v_src[0], sg_src[0], state, gn_g.reshape(1, dv))


def _mm_norm_kernel(*refs, n_k, emit_next, sub):
    if emit_next:
        a_ref, w_ref, res_ref, nw_ref, nw2_ref, o_ref, h_ref, acc_ref = refs
    else:
        a_ref, w_ref, res_ref, nw_ref, o_ref, acc_ref = refs
    kk = pl.program_id(1)
    tm = a_ref.shape[0]

    if n_k > 1:
        @pl.when(kk == 0)
        def _():
            acc_ref[...] = _dot(a_ref[...], w_ref[...])
    if n_k > 2:
        @pl.when((kk > 0) & (kk < n_k - 1))
        def _():
            acc_ref[...] += _dot(a_ref[...], w_ref[...])

    @pl.when(kk == n_k - 1)
    def _():
        for s in range(tm // sub):
            rs = slice(s * sub, (s + 1) * sub)
            m = _dot(a_ref[rs, :], w_ref[...])
            if n_k > 1:
                m = m + acc_ref[rs, :]
            inv = lax.rsqrt(jnp.mean(m * m, axis=-1, keepdims=True) + EPS)
            y = res_ref[rs, :] + m * inv * nw_ref[...]
            o_ref[rs, :] = y
            if emit_next:
                inv2 = lax.rsqrt(jnp.mean(y * y, axis=-1, keepdims=True) + EPS)
                h_ref[rs, :] = (y * inv2 * nw2_ref[...]).astype(h_ref.dtype)


def _matmul_norm_residual(a, w, res, norm_w, next_norm_w, tm, tk, name):
    rows, k = a.shape
    d = w.shape[1]
    n_k = k // tk
    emit_next = next_norm_w is not None
    row_i = lambda i, kk: (i, 0)
    const2 = lambda i, kk: (0, 0)
    in_specs = [pl.BlockSpec((tm, tk), lambda i, kk: (i, kk)),
                pl.BlockSpec((tk, d), lambda i, kk: (kk, 0)),
                pl.BlockSpec((tm, d), row_i),
                pl.BlockSpec((1, d), const2)]
    args = [a, w, res, norm_w.reshape(1, d)]
    out_specs = [pl.BlockSpec((tm, d), row_i)]
    out_shape = [jax.ShapeDtypeStruct((rows, d), F32)]
    acc_shape = (tm, d) if n_k > 1 else (SUBLANES, LANES)
    est = (2 * tm * tk * 2 + 2 * tk * d * 2 + 4 * tm * d * 4 + _nbytes(acc_shape, F32)
           + 4 * NORM_SUB_ROWS * d * 4)
    if emit_next:
        in_specs.append(pl.BlockSpec((1, d), const2))
        args.append(next_norm_w.reshape(1, d))
        out_specs.append(pl.BlockSpec((tm, d), row_i))
        out_shape.append(jax.ShapeDtypeStruct((rows, d), BF16))
        est += 2 * tm * d * 2
    outs = pl.pallas_call(
        functools.partial(_mm_norm_kernel, n_k=n_k, emit_next=emit_next, sub=min(tm, NORM_SUB_ROWS)),
        grid=(rows // tm, n_k),
        in_specs=in_specs,
        out_specs=out_specs,
        out_shape=out_shape,
        scratch_shapes=[pltpu.VMEM(acc_shape, F32)],
        compiler_params=_params(est),
        name=name,
    )(*args)
    return outs if emit_next else outs[0]


def _trunk_layer(x, conv_hist, ret_state, start, p):
    bsz, seq, d = x.shape
    rows = bsz * seq
    x2 = x.reshape(rows, d)
    d_qk = N_RET_HEADS * RET_DK
    d_v = N_RET_HEADS * RET_DV
    off_qk = 2 * d
    off_v = off_qk + 2 * d_qk
    off_sg = off_v + d_v
    off_gate = off_sg + d_v

    tm = min(rows, 1024)
    tm_big = 2 * tm if rows % (2 * tm) == 0 else tm
    tn = 1024
    w_in = p["w_in"]

    h = _rms_norm_bf16(x2, p["norm_mix_pre"], min(rows, 512))
    n_tab = max(seq, tm_big)
    cos_t, sin_t = _rope_tables(start, seq, n_tab, min(n_tab, 1024))

    a = _matmul(h, [(w_in, 0), (w_in, d)], [], _epi_glu, d, F32, tm_big, 512, "in_proj_glu")
    if seq % tm_big == 0:
        c_pre, big = _in_proj_conv(h, w_in, a, conv_hist, p["conv_dw"], p["conv_dw_b"],
                                   cos_t, sin_t, p["b_gate"], seq, tm_big)
        q_src, k_src = (big, 0), (big, d_qk)
        v_src, sg_src = (big, 2 * d_qk), (big, 2 * d_qk + d_v)
        gate_c_src, gate_r_src = (big, 2 * d_qk + 2 * d_v), (big, 2 * d_qk + 2 * d_v + d)
    else:
        c_pre = _conv_module(a.reshape(bsz, seq, d), conv_hist, p["conv_dw"], p["conv_dw_b"],
                             min(seq, 128)).reshape(rows, d)
        tab_tiles = n_tab // tm
        tab_spec = pl.BlockSpec((tm, ROPE_HALF), lambda i, j: (i % tab_tiles, 0))
        qk = _matmul(h, [(w_in, off_qk)], [(cos_t, tab_spec), (sin_t, tab_spec)],
                     functools.partial(_epi_rotary, n_q_tiles=d_qk // tn),
                     2 * d_qk, BF16, tm, tn, "in_proj_qk")
        v = _matmul(h, [(w_in, off_v)], [], _epi_identity, d_v, BF16, tm, tn, "in_proj_v")
        sg = _matmul(h, [(w_in, off_sg)], [], _epi_silu, d_v, BF16, tm, tn, "in_proj_swish_gate")
        gates = _matmul(h, [(w_in, off_gate)], [(p["b_gate"].reshape(1, 2 * d), _row_spec(tn, 0))],
                        _epi_sigmoid_bias, 2 * d, BF16, tm, tn, "in_proj_merge_gates")
        q_src, k_src, v_src, sg_src = (qk, 0), (qk, d_qk), (v, 0), (sg, 0)
        gate_c_src, gate_r_src = (gates, 0), (gates, d)

    new_hist = a.reshape(bsz, seq, d)[:, seq - (CONV_WIDTH - 1):, :]
    chunk = min(seq, 256)
    o, s_new = _retention(q_src, k_src, v_src, sg_src, ret_state, p["ret_gn_g"], bsz, seq, chunk)

    tm_n = min(rows, 512)
    y_c = _conv_out_proj(c_pre, p["conv_ln_g"], p["conv_ln_b"], p["w_conv_out"], gate_c_src, tm_n, tn)
    mix_in = _matmul(o, [(p["w_ret_out"], 0)],
                     [(gate_r_src[0], _tile_spec(tm, tn, gate_r_src[1])), (y_c, _tile_spec(tm, tn, 0))],
                     _epi_gate_mul_add, d, BF16, tm, tn, "ret_out_proj")

    x1, h2 = _matmul_norm_residual(mix_in, p["w_out"], x2, p["norm_mix_post"], p["norm_ffn_pre"],
                                   tm_n, d, "out_proj_norm")

    d_ff = p["w_ffn_gate"].shape[1]
    f = _matmul(h2, [(p["w_ffn_gate"], 0), (p["w_ffn_up"], 0)], [], _epi_swiglu,
                d_ff, BF16, tm_big, 512, "ffn_up")
    y = _matmul_norm_residual(f, p["w_ffn_down"], x1, p["norm_ffn_post"], None, tm_n, d_ff // 2, "ffn_down_norm")
    return y.reshape(bsz, seq, d), new_hist, s_new


def kernel(x_prompt, x_sample, cache_conv, state_ret, norm_mix_pre, norm_mix_post, w_in, b_gate, conv_dw, conv_dw_b, conv_ln_g, conv_ln_b, w_conv_out, ret_gn_g, w_ret_out, w_out, norm_ffn_pre, norm_ffn_post, w_ffn_gate, w_ffn_up, w_ffn_down):
    depth = w_in.shape[0]
    past_len = 1024
    y_prompt, y_sample = x_prompt, x_sample
    conv_p, ret_p, conv_s, ret_s = [], [], [], []
    n_batch = x_prompt.shape[0]
    for l in range(depth):
        p = {
            "norm_mix_pre": norm_mix_pre[l], "norm_mix_post": norm_mix_post[l],
            "w_in": w_in[l].astype(BF16), "b_gate": b_gate[l],
            "conv_dw": conv_dw[l], "conv_dw_b": conv_dw_b[l],
            "conv_ln_g": conv_ln_g[l], "conv_ln_b": conv_ln_b[l],
            "w_conv_out": w_conv_out[l].astype(BF16), "ret_gn_g": ret_gn_g[l],
            "w_ret_out": w_ret_out[l].astype(BF16), "w_out": w_out[l].astype(BF16),
            "norm_ffn_pre": norm_ffn_pre[l], "norm_ffn_post": norm_ffn_post[l],
            "w_ffn_gate": w_ffn_gate[l].astype(BF16), "w_ffn_up": w_ffn_up[l].astype(BF16),
            "w_ffn_down": w_ffn_down[l].astype(BF16),
        }
        zero_hist = jnp.zeros((n_batch, CONV_WIDTH - 1, x_prompt.shape[-1]), x_prompt.dtype)
        zero_state = jnp.zeros((n_batch, N_RET_HEADS, RET_DK, RET_DV), x_prompt.dtype)
        y_prompt, hp, sp = _trunk_layer(y_prompt, zero_hist, zero_state, 0, p)
        y_sample, hs, ss = _trunk_layer(y_sample, cache_conv[l], state_ret[l], past_len, p)
        conv_p.append(hp)
        ret_p.append(sp)
        conv_s.append(hs)
        ret_s.append(ss)
    return (y_prompt, y_sample, jnp.stack(conv_p), jnp.stack(ret_p), jnp.stack(conv_s), jnp.stack(ret_s))
```

```python
import functools
import math

import jax
import jax.numpy as jnp
from jax import lax
from jax.experimental import pallas as pl
from jax.experimental.pallas import tpu as pltpu

F32 = jnp.float32
BF16 = jnp.bfloat16

EPS = 1e-6
ROPE_BASE = 10000.0
CONV_WIDTH = 31
N_RET_HEADS = 8
RET_DK = 256
RET_DV = 512
ROPE_HALF = RET_DK // 2

MIB = 1024 * 1024
V7X_VMEM_REQUEST_CAP = 56 * MIB
SUBLANES = 8
LANES = 128
CONV_HALO_ROWS = 32
CONV_LEAD = CONV_HALO_ROWS - (CONV_WIDTH - 1)
CONV_ROW_BLK = 64
NORM_SUB_ROWS = 128
MM_SUB_ROWS = 256
IN_PROJ_UNROLL = 2
IN_PROJ_TILE = 1024


def _params(est_bytes):
    limit = min(V7X_VMEM_REQUEST_CAP, max(32 * MIB, int(est_bytes * 1.25)))
    return pltpu.CompilerParams(vmem_limit_bytes=limit)


def _nbytes(shape, dtype):
    return math.prod(shape) * jnp.dtype(dtype).itemsize


def _sigmoid(x):
    return jax.nn.sigmoid(x)


def _silu(x):
    return x * jax.nn.sigmoid(x)


def _dot(a, b):
    return jnp.dot(a, b, preferred_element_type=F32)


def _rms_kernel(x_ref, w_ref, o_ref):
    x = x_ref[...]
    ms = jnp.mean(x * x, axis=-1, keepdims=True)
    o_ref[...] = (x * lax.rsqrt(ms + EPS) * w_ref[...]).astype(o_ref.dtype)


def _rms_norm_bf16(x, w, tm):
    rows, d = x.shape
    est = 2 * tm * d * 4 + 2 * tm * d * 2
    return pl.pallas_call(
        _rms_kernel,
        grid=(rows // tm,),
        in_specs=[pl.BlockSpec((tm, d), lambda i: (i, 0)),
                  pl.BlockSpec((1, d), lambda i: (0, 0))],
        out_specs=pl.BlockSpec((tm, d), lambda i: (i, 0)),
        out_shape=jax.ShapeDtypeStruct((rows, d), BF16),
        compiler_params=_params(est),
        name="rms_norm",
    )(x, w.reshape(1, d))


def _mm_kernel(*refs, n_w, epilogue, sub):
    a_ref = refs[0]
    w_refs = refs[1:1 + n_w]
    extra_refs = refs[1 + n_w:-1]
    o_ref = refs[-1]
    for s in range(a_ref.shape[0] // sub):
        rs = slice(s * sub, (s + 1) * sub)
        a = a_ref[rs, :]
        accs = [_dot(a, w_ref[...]) for w_ref in w_refs]
        o_ref[rs, :] = epilogue(accs, extra_refs, rs).astype(o_ref.dtype)


def _matmul(a, ws, extras, epilogue, n_out, out_dtype, tm, tn, name):
    rows, k = a.shape
    grid = (rows // tm, n_out // tn)
    in_specs = [pl.BlockSpec((tm, k), lambda i, j: (i, 0))]
    args = [a]
    est = 2 * tm * k * 2
    for w, off in ws:
        assert off % tn == 0
        in_specs.append(pl.BlockSpec((k, tn), functools.partial(lambda i, j, o: (0, o + j), o=off // tn)))
        args.append(w)
        est += 2 * k * tn * 2
    for arr, spec in extras:
        in_specs.append(spec)
        args.append(arr)
        est += 2 * _nbytes(spec.block_shape, arr.dtype)
    est += 2 * tm * tn * jnp.dtype(out_dtype).itemsize
    est += (len(ws) + 1) * tm * tn * 4
    return pl.pallas_call(
        functools.partial(_mm_kernel, n_w=len(ws), epilogue=epilogue, sub=min(tm, MM_SUB_ROWS)),
        grid=grid,
        in_specs=in_specs,
        out_specs=pl.BlockSpec((tm, tn), lambda i, j: (i, j)),
        out_shape=jax.ShapeDtypeStruct((rows, n_out), out_dtype),
        compiler_params=_params(est),
        name=name,
    )(*args)


def _tile_spec(tm, tn, off):
    assert off % tn == 0
    return pl.BlockSpec((tm, tn), functools.partial(lambda i, j, o: (i, o + j), o=off // tn))


def _row_spec(tn, off):
    return pl.BlockSpec((1, tn), functools.partial(lambda i, j, o: (0, o + j), o=off // tn))


def _rotary(acc, cos, sin, scale):
    outs = []
    for h in range(acc.shape[-1] // RET_DK):
        x1 = acc[:, h * RET_DK:h * RET_DK + ROPE_HALF]
        x2 = acc[:, h * RET_DK + ROPE_HALF:(h + 1) * RET_DK]
        outs.append((x1 * cos - x2 * sin) * scale)
        outs.append((x1 * sin + x2 * cos) * scale)
    return jnp.concatenate(outs, axis=-1)


def _epi_glu(accs, extras, rs):
    return accs[0] * _sigmoid(accs[1])


def _epi_identity(accs, extras, rs):
    return accs[0]


def _epi_silu(accs, extras, rs):
    return _silu(accs[0])


def _epi_sigmoid_bias(accs, extras, rs):
    return _sigmoid(accs[0] + extras[0][...])


def _epi_rotary(accs, extras, rs, *, n_q_tiles):
    scale = jnp.where(pl.program_id(1) >= n_q_tiles, RET_DK ** -0.5, 1.0).astype(F32)
    return _rotary(accs[0], extras[0][rs, :], extras[1][rs, :], scale)


def _epi_gate_mul_add(accs, extras, rs):
    return extras[0][rs, :].astype(F32) * accs[0] + extras[1][rs, :]


def _epi_swiglu(accs, extras, rs):
    return _silu(accs[0]) * accs[1]


def _rope_kernel(inv_ref, cos_ref, sin_ref, *, start, seq_len, tt):
    row = pl.program_id(0) * tt + lax.broadcasted_iota(jnp.int32, (tt, ROPE_HALF), 0)
    pos = (start + lax.rem(row, seq_len)).astype(F32)
    ang = pos * inv_ref[...]
    cos_ref[...] = jnp.cos(ang)
    sin_ref[...] = jnp.sin(ang)


def _rope_tables(start, seq_len, n_rows, tt):
    inv_freq = 1.0 / (ROPE_BASE ** (jnp.arange(ROPE_HALF, dtype=F32) / ROPE_HALF))
    shape = jax.ShapeDtypeStruct((n_rows, ROPE_HALF), F32)
    return pl.pallas_call(
        functools.partial(_rope_kernel, start=start, seq_len=seq_len, tt=tt),
        grid=(n_rows // tt,),
        in_specs=[pl.BlockSpec((1, ROPE_HALF), lambda i: (0, 0))],
        out_specs=[pl.BlockSpec((tt, ROPE_HALF), lambda i: (i, 0))] * 2,
        out_shape=[shape, shape],
        name="rope_tables",
    )(inv_freq.reshape(1, ROPE_HALF))


def _conv_taps(x, w_row, bias, n_rows):
    acc = None
    for r in range(SUBLANES):
        rows = n_rows if r == 0 else n_rows + SUBLANES
        g = None
        for p in range((CONV_WIDTH + CONV_LEAD + SUBLANES - 1) // SUBLANES):
            j = SUBLANES * p + r - CONV_LEAD
            if 0 <= j < CONV_WIDTH:
                term = w_row(j) * x[SUBLANES * p:SUBLANES * p + rows]
                g = term if g is None else g + term
        shifted = g[r:r + n_rows]
        acc = shifted if acc is None else acc + shifted
    return acc + bias


def _conv_kernel(cur_ref, prev_ref, hist_ref, w_ref, b_ref, o_ref, win_ref, *, tt, row_blk):
    t = pl.program_id(1)

    @pl.when(t == 0)
    def _():
        win_ref[CONV_LEAD:CONV_HALO_ROWS, :] = hist_ref[0]

    @pl.when(t > 0)
    def _():
        win_ref[0:CONV_HALO_ROWS, :] = prev_ref[0]

    win_ref[CONV_HALO_ROWS:CONV_HALO_ROWS + tt, :] = cur_ref[0]

    d = cur_ref.shape[-1]
    for cb in range(d // LANES):
        cs = slice(cb * LANES, (cb + 1) * LANES)
        for rb in range(tt // row_blk):
            x = win_ref[rb * row_blk:rb * row_blk + row_blk + CONV_HALO_ROWS, cs]
            o_ref[0, rb * row_blk:(rb + 1) * row_blk, cs] = _conv_taps(
                x, lambda j: w_ref[j:j + 1, cs], b_ref[:, cs], row_blk)


def _conv_module(a, hist, conv_dw, conv_dw_b, tt):
    bsz, seq, d = a.shape
    n_hist = CONV_WIDTH - 1
    halo_per_tile = tt // CONV_HALO_ROWS
    row_blk = min(tt, CONV_ROW_BLK)
    est = 2 * tt * d * 4 + 4 * CONV_HALO_ROWS * d * 4 + 2 * tt * d * 4 + (tt + CONV_HALO_ROWS) * d * 4
    const2 = lambda b, t: (0, 0)
    return pl.pallas_call(
        functools.partial(_conv_kernel, tt=tt, row_blk=row_blk),
        grid=(bsz, seq // tt),
        in_specs=[
            pl.BlockSpec((1, tt, d), lambda b, t: (b, t, 0)),
            pl.BlockSpec((1, CONV_HALO_ROWS, d),
                         lambda b, t: (b, jnp.maximum(t * halo_per_tile - 1, 0), 0)),
            pl.BlockSpec((1, n_hist, d), lambda b, t: (b, 0, 0)),
            pl.BlockSpec((CONV_WIDTH, d), const2),
            pl.BlockSpec((1, d), const2),
        ],
        out_specs=pl.BlockSpec((1, tt, d), lambda b, t: (b, t, 0)),
        out_shape=jax.ShapeDtypeStruct((bsz, seq, d), F32),
        scratch_shapes=[pltpu.VMEM((CONV_HALO_ROWS + tt, d), F32)],
        compiler_params=_params(est),
        name="conv_module",
    )(a, a, hist, conv_dw, conv_dw_b.reshape(1, d))


def _in_proj_conv_kernel(h_ref, w_ref, acur_ref, aprev_ref, hist_ref, cw_ref, cb_ref, cos_ref, sin_ref,
                         bg_ref, cpre_ref, big_ref, win_ref, *, tm, tiles_per_seq, bounds):
    i = pl.program_id(0)
    t = pl.program_id(1)
    k_start, v_start, sg_start, gate_start = bounds
    n_sub = tm // MM_SUB_ROWS
    conv_blks_per_sub = (tm // CONV_ROW_BLK) // n_sub

    @pl.when(i % tiles_per_seq == 0)
    def _():
        win_ref[CONV_LEAD:CONV_HALO_ROWS, :] = hist_ref[0, t]

    @pl.when(i % tiles_per_seq != 0)
    def _():
        win_ref[0:CONV_HALO_ROWS, :] = aprev_ref[...]

    win_ref[CONV_HALO_ROWS:CONV_HALO_ROWS + tm, :] = acur_ref[...]

    def step(epilogue):
        half = w_ref.shape[-1] // 2
        conv_blks_per_half = conv_blks_per_sub // 2

        def sub_tiles(it, carry):
            for u in range(IN_PROJ_UNROLL):
                r_sub = pl.multiple_of((it * IN_PROJ_UNROLL + u) * MM_SUB_ROWS, MM_SUB_ROWS)
                rs = pl.ds(r_sub, MM_SUB_ROWS)
                for c in range(2):
                    cs = slice(c * half, (c + 1) * half)
                    acc = _dot(h_ref[rs, :], w_ref[:, cs])
                    for b in range(conv_blks_per_half):
                        r0 = pl.multiple_of(r_sub + (c * conv_blks_per_half + b) * CONV_ROW_BLK, CONV_ROW_BLK)
                        x = win_ref[pl.ds(r0, CONV_ROW_BLK + CONV_HALO_ROWS), :]
                        cpre_ref[pl.ds(r0, CONV_ROW_BLK), :] = _conv_taps(
                            x, lambda j: cw_ref[t, j:j + 1, :], cb_ref[t], CONV_ROW_BLK)
                    big_ref[rs, cs] = epilogue(acc, rs, cs).astype(big_ref.dtype)
            return carry

        lax.fori_loop(0, n_sub // IN_PROJ_UNROLL, sub_tiles, 0)

    def epi_rotary(acc, rs, cs):
        scale = jnp.where(t >= k_start, RET_DK ** -0.5, 1.0).astype(F32)
        return _rotary(acc, cos_ref[rs, :], sin_ref[rs, :], scale)

    pl.when(t < v_start)(lambda: step(epi_rotary))
    pl.when((t >= v_start) & (t < sg_start))(lambda: step(lambda acc, rs, cs: acc))
    pl.when((t >= sg_start) & (t < gate_start))(lambda: step(lambda acc, rs, cs: _silu(acc)))
    pl.when(t >= gate_start)(lambda: step(lambda acc, rs, cs: _sigmoid(acc + bg_ref[t - gate_start, :, cs])))


def _in_proj_conv(h, w_in, a, hist, conv_dw, conv_dw_b, cos_t, sin_t, b_gate, seq, tm):
    rows, d = h.shape
    tn = IN_PROJ_TILE
    n_chunks = d // LANES
    d_qk = N_RET_HEADS * RET_DK
    d_v = N_RET_HEADS * RET_DV
    off_wide = 2 * d
    wide_cols = 2 * d_qk + 2 * d_v + 2 * d
    assert wide_cols // tn == n_chunks and seq % tm == 0 and tm % CONV_HALO_ROWS == 0
    tiles_per_seq = seq // tm
    tab_tiles = cos_t.shape[0] // tm
    halo_per_tile = tm // CONV_HALO_ROWS
    bounds = (d_qk // tn, 2 * d_qk // tn, (2 * d_qk + d_v) // tn, (2 * d_qk + 2 * d_v) // tn)
    n_bias_steps = 2 * d // tn
    whole3 = lambda i, t: (0, 0, 0)
    tab_map = lambda i, t: (i % tab_tiles, 0)
    cw = conv_dw.reshape(CONV_WIDTH, n_chunks, LANES).transpose(1, 0, 2)
    cb = conv_dw_b.reshape(n_chunks, 1, LANES)
    hist_cm = hist.reshape(hist.shape[0], CONV_WIDTH - 1, n_chunks, LANES).transpose(0, 2, 1, 3)
    est = (2 * tm * d * 2 + 2 * d * tn * 2 + 4 * tm * LANES * 4 + 4 * tm * ROPE_HALF * 4
           + 2 * tm * tn * 2 + (tm + CONV_HALO_ROWS) * LANES * 4 + 3 * MM_SUB_ROWS * tn * 4)
    return pl.pallas_call(
        functools.partial(_in_proj_conv_kernel, tm=tm, tiles_per_seq=tiles_per_seq, bounds=bounds),
        grid=(rows // tm, n_chunks),
        in_specs=[
            pl.BlockSpec((tm, d), lambda i, t: (i, 0)),
            pl.BlockSpec((d, tn), lambda i, t: (0, off_wide // tn + t)),
            pl.BlockSpec((tm, LANES), lambda i, t: (i, t)),
            pl.BlockSpec((CONV_HALO_ROWS, LANES), lambda i, t: (jnp.maximum(i * halo_per_tile - 1, 0), t)),
            pl.BlockSpec((1, n_chunks, CONV_WIDTH - 1, LANES), lambda i, t: (i // tiles_per_seq, 0, 0, 0)),
            pl.BlockSpec((n_chunks, CONV_WIDTH, LANES), whole3),
            pl.BlockSpec((n_chunks, 1, LANES), whole3),
            pl.BlockSpec((tm, ROPE_HALF), tab_map),
            pl.BlockSpec((tm, ROPE_HALF), tab_map),
            pl.BlockSpec((n_bias_steps, 1, tn), whole3),
        ],
        out_specs=[
            pl.BlockSpec((tm, LANES), lambda i, t: (i, t)),
            pl.BlockSpec((tm, tn), lambda i, t: (i, t)),
        ],
        out_shape=[jax.ShapeDtypeStruct((rows, d), F32),
                   jax.ShapeDtypeStruct((rows, wide_cols), BF16)],
        scratch_shapes=[pltpu.VMEM((CONV_HALO_ROWS + tm, LANES), F32)],
        compiler_params=_params(est),
        name="in_proj_conv",
    )(h, w_in, a, a, hist_cm, cw, cb, cos_t, sin_t, b_gate.reshape(n_bias_steps, 1, tn))


def _ln_mm_kernel(c_ref, g_ref, b_ref, w_ref, gate_ref, o_ref, c_scr, *, sub):
    j = pl.program_id(1)
    tm = c_ref.shape[0]

    @pl.when(j == 0)
    def _():
        for s in range(tm // sub):
            rs = slice(s * sub, (s + 1) * sub)
            c = c_ref[rs, :]
            mu = jnp.mean(c, axis=-1, keepdims=True)
            cc = c - mu
            var = jnp.mean(cc * cc, axis=-1, keepdims=True)
            y = cc * lax.rsqrt(var + EPS) * g_ref[...] + b_ref[...]
            act = _silu(y).astype(c_scr.dtype)
            c_scr[rs, :] = act
            o_ref[rs, :] = gate_ref[rs, :].astype(F32) * _dot(act, w_ref[...])

    @pl.when(j > 0)
    def _():
        o_ref[...] = gate_ref[...].astype(F32) * _dot(c_scr[...], w_ref[...])


def _conv_out_proj(c_pre, ln_g, ln_b, w, gate_src, tm, tn):
    rows, d = c_pre.shape
    n_out = w.shape[1]
    gates, gate_off = gate_src
    est = 2 * tm * d * 4 + 2 * d * tn * 2 + 2 * tm * tn * 2 + 2 * tm * tn * 4 + tm * d * 2 + 6 * NORM_SUB_ROWS * d * 4
    const2 = lambda i, j: (0, 0)
    return pl.pallas_call(
        functools.partial(_ln_mm_kernel, sub=min(tm, NORM_SUB_ROWS)),
        grid=(rows // tm, n_out // tn),
        in_specs=[pl.BlockSpec((tm, d), lambda i, j: (i, 0)),
                  pl.BlockSpec((1, d), const2),
                  pl.BlockSpec((1, d), const2),
                  pl.BlockSpec((d, tn), lambda i, j: (0, j)),
                  _tile_spec(tm, tn, gate_off)],
        out_specs=pl.BlockSpec((tm, tn), lambda i, j: (i, j)),
        out_shape=jax.ShapeDtypeStruct((rows, n_out), F32),
        scratch_shapes=[pltpu.VMEM((tm, d), BF16)],
        compiler_params=_params(est),
        name="conv_out_proj",
    )(c_pre, ln_g.reshape(1, d), ln_b.reshape(1, d), w, gates)


def _log_decay(h):
    return math.log1p(-(2.0 ** (-5.0 - h)))


def _retention_kernel(q_ref, k_ref, v_ref, sg_ref, s0_ref, gn_ref, o_ref, s_out_ref,
                      s_ref, decay_ref, *, chunk, n_chunks):
    b = pl.program_id(0)
    c = pl.program_id(1)

    @pl.when((b == 0) & (c == 0))
    def _():
        n = lax.broadcasted_iota(jnp.int32, (chunk, chunk), 0)
        m = lax.broadcasted_iota(jnp.int32, (chunk, chunk), 1)
        diff = (n - m).astype(F32)
        for h in range(N_RET_HEADS):
            decay_ref[h] = jnp.where(diff >= 0.0, jnp.exp(jnp.maximum(diff, 0.0) * _log_decay(h)), 0.0)

    @pl.when(c == 0)
    def _():
        s_ref[...] = s0_ref[0]

    idx = lax.broadcasted_iota(jnp.int32, (chunk, 1), 0).astype(F32)
    for h in range(N_RET_HEADS):
        lg = _log_decay(h)
        q = q_ref[:, h * RET_DK:(h + 1) * RET_DK]
        k = k_ref[:, h * RET_DK:(h + 1) * RET_DK]
        v = v_ref[:, h * RET_DV:(h + 1) * RET_DV]
        s_old = s_ref[h]
        scores = lax.dot_general(q, k, (((1,), (1,)), ((), ())), preferred_element_type=F32)
        scores = (scores * decay_ref[h]).astype(BF16)
        o = _dot(scores, v)
        xi = jnp.exp((idx + 1.0) * lg)
        o = o + _dot(q, s_old.astype(BF16)) * xi
        zeta = jnp.exp((chunk - 1.0 - idx) * lg)
        kz = (k.astype(F32) * zeta).astype(BF16)
        s_ref[h] = math.exp(chunk * lg) * s_old + lax.dot_general(
            kz, v, (((0,), (0,)), ((), ())), preferred_element_type=F32)
        mu = jnp.mean(o, axis=-1, keepdims=True)
        oc = o - mu
        var = jnp.mean(oc * oc, axis=-1, keepdims=True)
        on = oc * lax.rsqrt(var + EPS) * gn_ref[:, h * RET_DV:(h + 1) * RET_DV]
        gate = sg_ref[:, h * RET_DV:(h + 1) * RET_DV].astype(F32)
        o_ref[:, h * RET_DV:(h + 1) * RET_DV] = (on * gate).astype(o_ref.dtype)

    @pl.when(c == n_chunks - 1)
    def _():
        s_out_ref[0] = s_ref[...]


def _retention(q_src, k_src, v_src, sg_src, state, gn_g, bsz, seq, chunk):
    rows = bsz * seq
    n_chunks = seq // chunk
    dqk = N_RET_HEADS * RET_DK
    dv = N_RET_HEADS * RET_DV
    state_blk = (1, N_RET_HEADS, RET_DK, RET_DV)
    est = (2 * 2 * chunk * dqk * 2 + 3 * 2 * chunk * dv * 2 + 4 * _nbytes(state_blk, F32)
           + _nbytes(state_blk, F32) + N_RET_HEADS * chunk * chunk * 4 + 8 * chunk * RET_DV * 4)

    def src_spec(src, width):
        assert src[1] % width == 0
        return pl.BlockSpec((chunk, width), functools.partial(
            lambda b, c, o: (b * n_chunks + c, o), o=src[1] // width))

    return pl.pallas_call(
        functools.partial(_retention_kernel, chunk=chunk, n_chunks=n_chunks),
        grid=(bsz, n_chunks),
        in_specs=[
            src_spec(q_src, dqk), src_spec(k_src, dqk), src_spec(v_src, dv), src_spec(sg_src, dv),
            pl.BlockSpec(state_blk, lambda b, c: (b, 0, 0, 0)),
            pl.BlockSpec((1, dv), lambda b, c: (0, 0)),
        ],
        out_specs=[pl.BlockSpec((chunk, dv), lambda b, c: (b * n_chunks + c, 0)),
                   pl.BlockSpec(state_blk, lambda b, c: (b, 0, 0, 0))],
        out_shape=[jax.ShapeDtypeStruct((rows, dv), BF16),
                   jax.ShapeDtypeStruct((bsz,) + state_blk[1:], F32)],
        scratch_shapes=[pltpu.VMEM(state_blk[1:], F32),
                        pltpu.VMEM((N_RET_HEADS, chunk, chunk), F32)],
        compiler_params=_params(est),
        name="retention",
    )(q_src[0], k_src[0], v_src[0], sg_src[0], state, gn_g.reshape(1, dv))


def _mm_norm_kernel(*refs, n_k, emit_next, sub):
    if emit_next:
        a_ref, w_ref, res_ref, nw_ref, nw2_ref, o_ref, h_ref, acc_ref = refs
    else:
        a_ref, w_ref, res_ref, nw_ref, o_ref, acc_ref = refs
    kk = pl.program_id(1)
    tm = a_ref.shape[0]

    if n_k > 1:
        @pl.when(kk == 0)
        def _():
            acc_ref[...] = _dot(a_ref[...], w_ref[...])
    if n_k > 2:
        @pl.when((kk > 0) & (kk < n_k - 1))
        def _():
            acc_ref[...] += _dot(a_ref[...], w_ref[...])

    @pl.when(kk == n_k - 1)
    def _():
        for s in range(tm // sub):
            rs = slice(s * sub, (s + 1) * sub)
            m = _dot(a_ref[rs, :], w_ref[...])
            if n_k > 1:
                m = m + acc_ref[rs, :]
            inv = lax.rsqrt(jnp.mean(m * m, axis=-1, keepdims=True) + EPS)
            y = res_ref[rs, :] + m * inv * nw_ref[...]
            o_ref[rs, :] = y
            if emit_next:
                inv2 = lax.rsqrt(jnp.mean(y * y, axis=-1, keepdims=True) + EPS)
                h_ref[rs, :] = (y * inv2 * nw2_ref[...]).astype(h_ref.dtype)


def _matmul_norm_residual(a, w, res, norm_w, next_norm_w, tm, tk, name):
    rows, k = a.shape
    d = w.shape[1]
    n_k = k // tk
    emit_next = next_norm_w is not None
    row_i = lambda i, kk: (i, 0)
    const2 = lambda i, kk: (0, 0)
    in_specs = [pl.BlockSpec((tm, tk), lambda i, kk: (i, kk)),
                pl.BlockSpec((tk, d), lambda i, kk: (kk, 0)),
                pl.BlockSpec((tm, d), row_i),
                pl.BlockSpec((1, d), const2)]
    args = [a, w, res, norm_w.reshape(1, d)]
    out_specs = [pl.BlockSpec((tm, d), row_i)]
    out_shape = [jax.ShapeDtypeStruct((rows, d), F32)]
    acc_shape = (tm, d) if n_k > 1 else (SUBLANES, LANES)
    est = (2 * tm * tk * 2 + 2 * tk * d * 2 + 4 * tm * d * 4 + _nbytes(acc_shape, F32)
           + 4 * NORM_SUB_ROWS * d * 4)
    if emit_next:
        in_specs.append(pl.BlockSpec((1, d), const2))
        args.append(next_norm_w.reshape(1, d))
        out_specs.append(pl.BlockSpec((tm, d), row_i))
        out_shape.append(jax.ShapeDtypeStruct((rows, d), BF16))
        est += 2 * tm * d * 2
    outs = pl.pallas_call(
        functools.partial(_mm_norm_kernel, n_k=n_k, emit_next=emit_next, sub=min(tm, NORM_SUB_ROWS)),
        grid=(rows // tm, n_k),
        in_specs=in_specs,
        out_specs=out_specs,
        out_shape=out_shape,
        scratch_shapes=[pltpu.VMEM(acc_shape, F32)],
        compiler_params=_params(est),
        name=name,
    )(*args)
    return outs if emit_next else outs[0]


def _trunk_layer(x, conv_hist, ret_state, start, p):
    bsz, seq, d = x.shape
    rows = bsz * seq
    x2 = x.reshape(rows, d)
    d_qk = N_RET_HEADS * RET_DK
    d_v = N_RET_HEADS * RET_DV
    off_qk = 2 * d
    off_v = off_qk + 2 * d_qk
    off_sg = off_v + d_v
    off_gate = off_sg + d_v

    tm = min(rows, 1024)
    tm_big = 2 * tm if rows % (2 * tm) == 0 else tm
    tn = 1024
    w_in = p["w_in"]

    h = _rms_norm_bf16(x2, p["norm_mix_pre"], min(rows, 512))
    n_tab = max(seq, tm_big)
    cos_t, sin_t = _rope_tables(start, seq, n_tab, min(n_tab, 1024))

    a = _matmul(h, [(w_in, 0), (w_in, d)], [], _epi_glu, d, F32, tm_big, 512, "in_proj_glu")
    if seq % tm_big == 0:
        c_pre, big = _in_proj_conv(h, w_in, a, conv_hist, p["conv_dw"], p["conv_dw_b"],
                                   cos_t, sin_t, p["b_gate"], seq, tm_big)
        q_src, k_src = (big, 0), (big, d_qk)
        v_src, sg_src = (big, 2 * d_qk), (big, 2 * d_qk + d_v)
        gate_c_src, gate_r_src = (big, 2 * d_qk + 2 * d_v), (big, 2 * d_qk + 2 * d_v + d)
    else:
        c_pre = _conv_module(a.reshape(bsz, seq, d), conv_hist, p["conv_dw"], p["conv_dw_b"],
                             min(seq, 128)).reshape(rows, d)
        tab_tiles = n_tab // tm
        tab_spec = pl.BlockSpec((tm, ROPE_HALF), lambda i, j: (i % tab_tiles, 0))
        qk = _matmul(h, [(w_in, off_qk)], [(cos_t, tab_spec), (sin_t, tab_spec)],
                     functools.partial(_epi_rotary, n_q_tiles=d_qk // tn),
                     2 * d_qk, BF16, tm, tn, "in_proj_qk")
        v = _matmul(h, [(w_in, off_v)], [], _epi_identity, d_v, BF16, tm, tn, "in_proj_v")
        sg = _matmul(h, [(w_in, off_sg)], [], _epi_silu, d_v, BF16, tm, tn, "in_proj_swish_gate")
        gates = _matmul(h, [(w_in, off_gate)], [(p["b_gate"].reshape(1, 2 * d), _row_spec(tn, 0))],
                        _epi_sigmoid_bias, 2 * d, BF16, tm, tn, "in_proj_merge_gates")
        q_src, k_src, v_src, sg_src = (qk, 0), (qk, d_qk), (v, 0), (sg, 0)
        gate_c_src, gate_r_src = (gates, 0), (gates, d)

    new_hist = a.reshape(bsz, seq, d)[:, seq - (CONV_WIDTH - 1):, :]
    chunk = min(seq, 256)
    o, s_new = _retention(q_src, k_src, v_src, sg_src, ret_state, p["ret_gn_g"], bsz, seq, chunk)

    tm_n = min(rows, 512)
    y_c = _conv_out_proj(c_pre, p["conv_ln_g"], p["conv_ln_b"], p["w_conv_out"], gate_c_src, tm_n, d)
    mix_in = _matmul(o, [(p["w_ret_out"], 0)],
                     [(gate_r_src[0], _tile_spec(tm, tn, gate_r_src[1])), (y_c, _tile_spec(tm, tn, 0))],
                     _epi_gate_mul_add, d, BF16, tm, tn, "ret_out_proj")

    x1, h2 = _matmul_norm_residual(mix_in, p["w_out"], x2, p["norm_mix_post"], p["norm_ffn_pre"],
                                   tm_n, d, "out_proj_norm")

    d_ff = p["w_ffn_gate"].shape[1]
    f = _matmul(h2, [(p["w_ffn_gate"], 0), (p["w_ffn_up"], 0)], [], _epi_swiglu,
                d_ff, BF16, tm_big, 512, "ffn_up")
    y = _matmul_norm_residual(f, p["w_ffn_down"], x1, p["norm_ffn_post"], None, tm_n, d_ff // 2, "ffn_down_norm")
    return y.reshape(bsz, seq, d), new_hist, s_new


def kernel(x_prompt, x_sample, cache_conv, state_ret, norm_mix_pre, norm_mix_post, w_in, b_gate, conv_dw, conv_dw_b, conv_ln_g, conv_ln_b, w_conv_out, ret_gn_g, w_ret_out, w_out, norm_ffn_pre, norm_ffn_post, w_ffn_gate, w_ffn_up, w_ffn_down):
    depth = w_in.shape[0]
    past_len = 1024
    y_prompt, y_sample = x_prompt, x_sample
    conv_p, ret_p, conv_s, ret_s = [], [], [], []
    n_batch = x_prompt.shape[0]
    for l in range(depth):
        p = {
            "norm_mix_pre": norm_mix_pre[l], "norm_mix_post": norm_mix_post[l],
            "w_in": w_in[l].astype(BF16), "b_gate": b_gate[l],
            "conv_dw": conv_dw[l], "conv_dw_b": conv_dw_b[l],
            "conv_ln_g": conv_ln_g[l], "conv_ln_b": conv_ln_b[l],
            "w_conv_out": w_conv_out[l].astype(BF16), "ret_gn_g": ret_gn_g[l],
            "w_ret_out": w_ret_out[l].astype(BF16), "w_out": w_out[l].astype(BF16),
            "norm_ffn_pre": norm_ffn_pre[l], "norm_ffn_post": norm_ffn_post[l],
            "w_ffn_gate": w_ffn_gate[l].astype(BF16), "w_ffn_up": w_ffn_up[l].astype(BF16),
            "w_ffn_down": w_ffn_down[l].astype(BF16),
        }
        zero_hist = jnp.zeros((n_batch, CONV_WIDTH - 1, x_prompt.shape[-1]), x_prompt.dtype)
        zero_state = jnp.zeros((n_batch, N_RET_HEADS, RET_DK, RET_DV), x_prompt.dtype)
        y_prompt, hp, sp = _trunk_layer(y_prompt, zero_hist, zero_state, 0, p)
        y_sample, hs, ss = _trunk_layer(y_sample, cache_conv[l], state_ret[l], past_len, p)
        conv_p.append(hp)
        ret_p.append(sp)
        conv_s.append(hs)
        ret_s.append(ss)
    return (y_prompt, y_sample, jnp.stack(conv_p), jnp.stack(ret_p), jnp.stack(conv_s), jnp.stack(ret_s))
```

```python
import functools
import math

import jax
import jax.numpy as jnp
from jax import lax
from jax.experimental import pallas as pl
from jax.experimental.pallas import tpu as pltpu

F32 = jnp.float32
BF16 = jnp.bfloat16

EPS = 1e-6
ROPE_BASE = 10000.0
CONV_WIDTH = 31
N_RET_HEADS = 8
RET_DK = 256
RET_DV = 512
ROPE_HALF = RET_DK // 2

MIB = 1024 * 1024
V7X_VMEM_REQUEST_CAP = 56 * MIB
SUBLANES = 8
LANES = 128
CONV_HALO_ROWS = 32
CONV_LEAD = CONV_HALO_ROWS - (CONV_WIDTH - 1)
CONV_ROW_BLK = 64
NORM_SUB_ROWS = 128
MM_SUB_ROWS = 256
IN_PROJ_UNROLL = 2
IN_PROJ_TILE = 1024


def _params(est_bytes):
    limit = min(V7X_VMEM_REQUEST_CAP, max(32 * MIB, int(est_bytes * 1.25)))
    return pltpu.CompilerParams(vmem_limit_bytes=limit)


def _nbytes(shape, dtype):
    return math.prod(shape) * jnp.dtype(dtype).itemsize


def _sigmoid(x):
    return jax.nn.sigmoid(x)


def _silu(x):
    return x * jax.nn.sigmoid(x)


def _dot(a, b):
    return jnp.dot(a, b, preferred_element_type=F32)


def _rms_glu_kernel(x_ref, nw_ref, wv_ref, wg_ref, a_ref, h_ref, *, sub):
    j = pl.program_id(1)
    tm = x_ref.shape[0]

    def glu(rs, h):
        a_ref[rs, :] = _dot(h, wv_ref[...]) * _sigmoid(_dot(h, wg_ref[...]))

    @pl.when(j == 0)
    def _():
        for s in range(tm // sub):
            rs = slice(s * sub, (s + 1) * sub)
            x = x_ref[rs, :]
            ms = jnp.mean(x * x, axis=-1, keepdims=True)
            h = (x * lax.rsqrt(ms + EPS) * nw_ref[...]).astype(h_ref.dtype)
            h_ref[rs, :] = h
            glu(rs, h)

    @pl.when(j > 0)
    def _():
        for s in range(tm // sub):
            rs = slice(s * sub, (s + 1) * sub)
            glu(rs, h_ref[rs, :])


def _rms_glu(x, norm_w, w_in, off_value, off_gate, tm, tn):
    rows, d = x.shape
    est = 2 * tm * d * 4 + 4 * d * tn * 2 + 2 * tm * tn * 4 + 2 * tm * d * 2 + 3 * MM_SUB_ROWS * tn * 4
    w_spec = lambda off: pl.BlockSpec((d, tn), functools.partial(lambda i, j, o: (0, o + j), o=off // tn))
    return pl.pallas_call(
        functools.partial(_rms_glu_kernel, sub=min(tm, MM_SUB_ROWS)),
        grid=(rows // tm, d // tn),
        in_specs=[pl.BlockSpec((tm, d), lambda i, j: (i, 0)),
                  pl.BlockSpec((1, d), lambda i, j: (0, 0)),
                  w_spec(off_value), w_spec(off_gate)],
        out_specs=[pl.BlockSpec((tm, tn), lambda i, j: (i, j)),
                   pl.BlockSpec((tm, d), lambda i, j: (i, 0))],
        out_shape=[jax.ShapeDtypeStruct((rows, d), F32),
                   jax.ShapeDtypeStruct((rows, d), BF16)],
        compiler_params=_params(est),
        name="rms_glu",
    )(x, norm_w.reshape(1, d), w_in, w_in)


def _mm_kernel(*refs, n_w, epilogue, sub):
    a_ref = refs[0]
    w_refs = refs[1:1 + n_w]
    extra_refs = refs[1 + n_w:-1]
    o_ref = refs[-1]
    for s in range(a_ref.shape[0] // sub):
        rs = slice(s * sub, (s + 1) * sub)
        a = a_ref[rs, :]
        accs = [_dot(a, w_ref[...]) for w_ref in w_refs]
        o_ref[rs, :] = epilogue(accs, extra_refs, rs).astype(o_ref.dtype)


def _matmul(a, ws, extras, epilogue, n_out, out_dtype, tm, tn, name):
    rows, k = a.shape
    grid = (rows // tm, n_out // tn)
    in_specs = [pl.BlockSpec((tm, k), lambda i, j: (i, 0))]
    args = [a]
    est = 2 * tm * k * 2
    for w, off in ws:
        assert off % tn == 0
        in_specs.append(pl.BlockSpec((k, tn), functools.partial(lambda i, j, o: (0, o + j), o=off // tn)))
        args.append(w)
        est += 2 * k * tn * 2
    for arr, spec in extras:
        in_specs.append(spec)
        args.append(arr)
        est += 2 * _nbytes(spec.block_shape, arr.dtype)
    est += 2 * tm * tn * jnp.dtype(out_dtype).itemsize
    est += (len(ws) + 1) * tm * tn * 4
    return pl.pallas_call(
        functools.partial(_mm_kernel, n_w=len(ws), epilogue=epilogue, sub=min(tm, MM_SUB_ROWS)),
        grid=grid,
        in_specs=in_specs,
        out_specs=pl.BlockSpec((tm, tn), lambda i, j: (i, j)),
        out_shape=jax.ShapeDtypeStruct((rows, n_out), out_dtype),
        compiler_params=_params(est),
        name=name,
    )(*args)


def _tile_spec(tm, tn, off):
    assert off % tn == 0
    return pl.BlockSpec((tm, tn), functools.partial(lambda i, j, o: (i, o + j), o=off // tn))


def _row_spec(tn, off):
    return pl.BlockSpec((1, tn), functools.partial(lambda i, j, o: (0, o + j), o=off // tn))


def _rotary(acc, cos, sin, scale):
    outs = []
    for h in range(acc.shape[-1] // RET_DK):
        x1 = acc[:, h * RET_DK:h * RET_DK + ROPE_HALF]
        x2 = acc[:, h * RET_DK + ROPE_HALF:(h + 1) * RET_DK]
        outs.append((x1 * cos - x2 * sin) * scale)
        outs.append((x1 * sin + x2 * cos) * scale)
    return jnp.concatenate(outs, axis=-1)


def _epi_identity(accs, extras, rs):
    return accs[0]


def _epi_silu(accs, extras, rs):
    return _silu(accs[0])


def _epi_sigmoid_bias(accs, extras, rs):
    return _sigmoid(accs[0] + extras[0][...])


def _epi_rotary(accs, extras, rs, *, n_q_tiles):
    scale = jnp.where(pl.program_id(1) >= n_q_tiles, RET_DK ** -0.5, 1.0).astype(F32)
    return _rotary(accs[0], extras[0][rs, :], extras[1][rs, :], scale)


def _epi_gate_mul_add(accs, extras, rs):
    return extras[0][rs, :].astype(F32) * accs[0] + extras[1][rs, :]


def _epi_swiglu(accs, extras, rs):
    return _silu(accs[0]) * accs[1]


def _rope_kernel(inv_ref, cos_ref, sin_ref, *, start, seq_len, tt):
    row = pl.program_id(0) * tt + lax.broadcasted_iota(jnp.int32, (tt, ROPE_HALF), 0)
    pos = (start + lax.rem(row, seq_len)).astype(F32)
    ang = pos * inv_ref[...]
    cos_ref[...] = jnp.cos(ang)
    sin_ref[...] = jnp.sin(ang)


def _rope_tables(start, seq_len, n_rows, tt):
    inv_freq = 1.0 / (ROPE_BASE ** (jnp.arange(ROPE_HALF, dtype=F32) / ROPE_HALF))
    shape = jax.ShapeDtypeStruct((n_rows, ROPE_HALF), F32)
    return pl.pallas_call(
        functools.partial(_rope_kernel, start=start, seq_len=seq_len, tt=tt),
        grid=(n_rows // tt,),
        in_specs=[pl.BlockSpec((1, ROPE_HALF), lambda i: (0, 0))],
        out_specs=[pl.BlockSpec((tt, ROPE_HALF), lambda i: (i, 0))] * 2,
        out_shape=[shape, shape],
        name="rope_tables",
    )(inv_freq.reshape(1, ROPE_HALF))


def _conv_taps(x, w_row, bias, n_rows):
    acc = None
    for r in range(SUBLANES):
        rows = n_rows if r == 0 else n_rows + SUBLANES
        g = None
        for p in range((CONV_WIDTH + CONV_LEAD + SUBLANES - 1) // SUBLANES):
            j = SUBLANES * p + r - CONV_LEAD
            if 0 <= j < CONV_WIDTH:
                term = w_row(j) * x[SUBLANES * p:SUBLANES * p + rows]
                g = term if g is None else g + term
        shifted = g[r:r + n_rows]
        acc = shifted if acc is None else acc + shifted
    return acc + bias


def _conv_kernel(cur_ref, prev_ref, hist_ref, w_ref, b_ref, o_ref, win_ref, *, tt, row_blk):
    t = pl.program_id(1)

    @pl.when(t == 0)
    def _():
        win_ref[CONV_LEAD:CONV_HALO_ROWS, :] = hist_ref[0]

    @pl.when(t > 0)
    def _():
        win_ref[0:CONV_HALO_ROWS, :] = prev_ref[0]

    win_ref[CONV_HALO_ROWS:CONV_HALO_ROWS + tt, :] = cur_ref[0]

    d = cur_ref.shape[-1]
    for cb in range(d // LANES):
        cs = slice(cb * LANES, (cb + 1) * LANES)
        for rb in range(tt // row_blk):
            x = win_ref[rb * row_blk:rb * row_blk + row_blk + CONV_HALO_ROWS, cs]
            o_ref[0, rb * row_blk:(rb + 1) * row_blk, cs] = _conv_taps(
                x, lambda j: w_ref[j:j + 1, cs], b_ref[:, cs], row_blk)


def _conv_module(a, hist, conv_dw, conv_dw_b, tt):
    bsz, seq, d = a.shape
    n_hist = CONV_WIDTH - 1
    halo_per_tile = tt // CONV_HALO_ROWS
    row_blk = min(tt, CONV_ROW_BLK)
    est = 2 * tt * d * 4 + 4 * CONV_HALO_ROWS * d * 4 + 2 * tt * d * 4 + (tt + CONV_HALO_ROWS) * d * 4
    const2 = lambda b, t: (0, 0)
    return pl.pallas_call(
        functools.partial(_conv_kernel, tt=tt, row_blk=row_blk),
        grid=(bsz, seq // tt),
        in_specs=[
            pl.BlockSpec((1, tt, d), lambda b, t: (b, t, 0)),
            pl.BlockSpec((1, CONV_HALO_ROWS, d),
                         lambda b, t: (b, jnp.maximum(t * halo_per_tile - 1, 0), 0)),
            pl.BlockSpec((1, n_hist, d), lambda b, t: (b, 0, 0)),
            pl.BlockSpec((CONV_WIDTH, d), const2),
            pl.BlockSpec((1, d), const2),
        ],
        out_specs=pl.BlockSpec((1, tt, d), lambda b, t: (b, t, 0)),
        out_shape=jax.ShapeDtypeStruct((bsz, seq, d), F32),
        scratch_shapes=[pltpu.VMEM((CONV_HALO_ROWS + tt, d), F32)],
        compiler_params=_params(est),
        name="conv_module",
    )(a, a, hist, conv_dw, conv_dw_b.reshape(1, d))


def _in_proj_conv_kernel(h_ref, w_ref, acur_ref, aprev_ref, hist_ref, cw_ref, cb_ref, cos_ref, sin_ref,
                         bg_ref, cpre_ref, big_ref, win_ref, *, tm, tiles_per_seq, bounds):
    i = pl.program_id(0)
    t = pl.program_id(1)
    k_start, v_start, sg_start, gate_start = bounds
    n_sub = tm // MM_SUB_ROWS
    conv_blks_per_sub = (tm // CONV_ROW_BLK) // n_sub

    @pl.when(i % tiles_per_seq == 0)
    def _():
        win_ref[CONV_LEAD:CONV_HALO_ROWS, :] = hist_ref[0, t]

    @pl.when(i % tiles_per_seq != 0)
    def _():
        win_ref[0:CONV_HALO_ROWS, :] = aprev_ref[...]

    win_ref[CONV_HALO_ROWS:CONV_HALO_ROWS + tm, :] = acur_ref[...]

    def step(epilogue):
        half = w_ref.shape[-1] // 2
        conv_blks_per_half = conv_blks_per_sub // 2

        def sub_tiles(it, carry):
            for u in range(IN_PROJ_UNROLL):
                r_sub = pl.multiple_of((it * IN_PROJ_UNROLL + u) * MM_SUB_ROWS, MM_SUB_ROWS)
                rs = pl.ds(r_sub, MM_SUB_ROWS)
                for c in range(2):
                    cs = slice(c * half, (c + 1) * half)
                    acc = _dot(h_ref[rs, :], w_ref[:, cs])
                    for b in range(conv_blks_per_half):
                        r0 = pl.multiple_of(r_sub + (c * conv_blks_per_half + b) * CONV_ROW_BLK, CONV_ROW_BLK)
                        x = win_ref[pl.ds(r0, CONV_ROW_BLK + CONV_HALO_ROWS), :]
                        cpre_ref[pl.ds(r0, CONV_ROW_BLK), :] = _conv_taps(
                            x, lambda j: cw_ref[t, j:j + 1, :], cb_ref[t], CONV_ROW_BLK)
                    big_ref[rs, cs] = epilogue(acc, rs, cs).astype(big_ref.dtype)
            return carry

        lax.fori_loop(0, n_sub // IN_PROJ_UNROLL, sub_tiles, 0)

    def epi_rotary(acc, rs, cs):
        scale = jnp.where(t >= k_start, RET_DK ** -0.5, 1.0).astype(F32)
        return _rotary(acc, cos_ref[rs, :], sin_ref[rs, :], scale)

    pl.when(t < v_start)(lambda: step(epi_rotary))
    pl.when((t >= v_start) & (t < sg_start))(lambda: step(lambda acc, rs, cs: acc))
    pl.when((t >= sg_start) & (t < gate_start))(lambda: step(lambda acc, rs, cs: _silu(acc)))
    pl.when(t >= gate_start)(lambda: step(lambda acc, rs, cs: _sigmoid(acc + bg_ref[t - gate_start, :, cs])))


def _in_proj_conv(h, w_in, a, hist, conv_dw, conv_dw_b, cos_t, sin_t, b_gate, seq, tm):
    rows, d = h.shape
    tn = IN_PROJ_TILE
    n_chunks = d // LANES
    d_qk = N_RET_HEADS * RET_DK
    d_v = N_RET_HEADS * RET_DV
    off_wide = 2 * d
    wide_cols = 2 * d_qk + 2 * d_v + 2 * d
    assert wide_cols // tn == n_chunks and seq % tm == 0 and tm % CONV_HALO_ROWS == 0
    tiles_per_seq = seq // tm
    tab_tiles = cos_t.shape[0] // tm
    halo_per_tile = tm // CONV_HALO_ROWS
    bounds = (d_qk // tn, 2 * d_qk // tn, (2 * d_qk + d_v) // tn, (2 * d_qk + 2 * d_v) // tn)
    n_bias_steps = 2 * d // tn
    whole3 = lambda i, t: (0, 0, 0)
    tab_map = lambda i, t: (i % tab_tiles, 0)
    cw = conv_dw.reshape(CONV_WIDTH, n_chunks, LANES).transpose(1, 0, 2)
    cb = conv_dw_b.reshape(n_chunks, 1, LANES)
    hist_cm = hist.reshape(hist.shape[0], CONV_WIDTH - 1, n_chunks, LANES).transpose(0, 2, 1, 3)
    est = (2 * tm * d * 2 + 2 * d * tn * 2 + 4 * tm * LANES * 4 + 4 * tm * ROPE_HALF * 4
           + 2 * tm * tn * 2 + (tm + CONV_HALO_ROWS) * LANES * 4 + 3 * MM_SUB_ROWS * tn * 4)
    return pl.pallas_call(
        functools.partial(_in_proj_conv_kernel, tm=tm, tiles_per_seq=tiles_per_seq, bounds=bounds),
        grid=(rows // tm, n_chunks),
        in_specs=[
            pl.BlockSpec((tm, d), lambda i, t: (i, 0)),
            pl.BlockSpec((d, tn), lambda i, t: (0, off_wide // tn + t)),
            pl.BlockSpec((tm, LANES), lambda i, t: (i, t)),
            pl.BlockSpec((CONV_HALO_ROWS, LANES), lambda i, t: (jnp.maximum(i * halo_per_tile - 1, 0), t)),
            pl.BlockSpec((1, n_chunks, CONV_WIDTH - 1, LANES), lambda i, t: (i // tiles_per_seq, 0, 0, 0)),
            pl.BlockSpec((n_chunks, CONV_WIDTH, LANES), whole3),
            pl.BlockSpec((n_chunks, 1, LANES), whole3),
            pl.BlockSpec((tm, ROPE_HALF), tab_map),
            pl.BlockSpec((tm, ROPE_HALF), tab_map),
            pl.BlockSpec((n_bias_steps, 1, tn), whole3),
        ],
        out_specs=[
            pl.BlockSpec((tm, LANES), lambda i, t: (i, t)),
            pl.BlockSpec((tm, tn), lambda i, t: (i, t)),
        ],
        out_shape=[jax.ShapeDtypeStruct((rows, d), F32),
                   jax.ShapeDtypeStruct((rows, wide_cols), BF16)],
        scratch_shapes=[pltpu.VMEM((CONV_HALO_ROWS + tm, LANES), F32)],
        compiler_params=_params(est),
        name="in_proj_conv",
    )(h, w_in, a, a, hist_cm, cw, cb, cos_t, sin_t, b_gate.reshape(n_bias_steps, 1, tn))


def _ln_mm_kernel(c_ref, g_ref, b_ref, w_ref, gate_ref, o_ref, c_scr, *, sub):
    j = pl.program_id(1)
    tm = c_ref.shape[0]

    @pl.when(j == 0)
    def _():
        for s in range(tm // sub):
            rs = slice(s * sub, (s + 1) * sub)
            c = c_ref[rs, :]
            mu = jnp.mean(c, axis=-1, keepdims=True)
            cc = c - mu
            var = jnp.mean(cc * cc, axis=-1, keepdims=True)
            y = cc * lax.rsqrt(var + EPS) * g_ref[...] + b_ref[...]
            act = _silu(y).astype(c_scr.dtype)
            c_scr[rs, :] = act
            o_ref[rs, :] = gate_ref[rs, :].astype(F32) * _dot(act, w_ref[...])

    @pl.when(j > 0)
    def _():
        o_ref[...] = gate_ref[...].astype(F32) * _dot(c_scr[...], w_ref[...])


def _conv_out_proj(c_pre, ln_g, ln_b, w, gate_src, tm, tn):
    rows, d = c_pre.shape
    n_out = w.shape[1]
    gates, gate_off = gate_src
    est = 2 * tm * d * 4 + 2 * d * tn * 2 + 2 * tm * tn * 2 + 2 * tm * tn * 4 + tm * d * 2 + 6 * NORM_SUB_ROWS * d * 4
    const2 = lambda i, j: (0, 0)
    return pl.pallas_call(
        functools.partial(_ln_mm_kernel, sub=min(tm, NORM_SUB_ROWS)),
        grid=(rows // tm, n_out // tn),
        in_specs=[pl.BlockSpec((tm, d), lambda i, j: (i, 0)),
                  pl.BlockSpec((1, d), const2),
                  pl.BlockSpec((1, d), const2),
                  pl.BlockSpec((d, tn), lambda i, j: (0, j)),
                  _tile_spec(tm, tn, gate_off)],
        out_specs=pl.BlockSpec((tm, tn), lambda i, j: (i, j)),
        out_shape=jax.ShapeDtypeStruct((rows, n_out), F32),
        scratch_shapes=[pltpu.VMEM((tm, d), BF16)],
        compiler_params=_params(est),
        name="conv_out_proj",
    )(c_pre, ln_g.reshape(1, d), ln_b.reshape(1, d), w, gates)


def _log_decay(h):
    return math.log1p(-(2.0 ** (-5.0 - h)))


def _retention_kernel(q_ref, k_ref, v_ref, sg_ref, s0_ref, gn_ref, o_ref, s_out_ref,
                      s_ref, decay_ref, *, chunk, n_chunks):
    b = pl.program_id(0)
    c = pl.program_id(1)

    @pl.when((b == 0) & (c == 0))
    def _():
        n = lax.broadcasted_iota(jnp.int32, (chunk, chunk), 0)
        m = lax.broadcasted_iota(jnp.int32, (chunk, chunk), 1)
        diff = (n - m).astype(F32)
        for h in range(N_RET_HEADS):
            decay_ref[h] = jnp.where(diff >= 0.0, jnp.exp(jnp.maximum(diff, 0.0) * _log_decay(h)), 0.0)

    @pl.when(c == 0)
    def _():
        s_ref[...] = s0_ref[0]

    idx = lax.broadcasted_iota(jnp.int32, (chunk, 1), 0).astype(F32)
    for h in range(N_RET_HEADS):
        lg = _log_decay(h)
        q = q_ref[:, h * RET_DK:(h + 1) * RET_DK]
        k = k_ref[:, h * RET_DK:(h + 1) * RET_DK]
        v = v_ref[:, h * RET_DV:(h + 1) * RET_DV]
        s_old = s_ref[h]
        scores = lax.dot_general(q, k, (((1,), (1,)), ((), ())), preferred_element_type=F32)
        scores = (scores * decay_ref[h]).astype(BF16)
        o = _dot(scores, v)
        xi = jnp.exp((idx + 1.0) * lg)
        o = o + _dot(q, s_old.astype(BF16)) * xi
        zeta = jnp.exp((chunk - 1.0 - idx) * lg)
        kz = (k.astype(F32) * zeta).astype(BF16)
        s_ref[h] = math.exp(chunk * lg) * s_old + lax.dot_general(
            kz, v, (((0,), (0,)), ((), ())), preferred_element_type=F32)
        mu = jnp.mean(o, axis=-1, keepdims=True)
        oc = o - mu
        var = jnp.mean(oc * oc, axis=-1, keepdims=True)
        on = oc * lax.rsqrt(var + EPS) * gn_ref[:, h * RET_DV:(h + 1) * RET_DV]
        gate = sg_ref[:, h * RET_DV:(h + 1) * RET_DV].astype(F32)
        o_ref[:, h * RET_DV:(h + 1) * RET_DV] = (on * gate).astype(o_ref.dtype)

    @pl.when(c == n_chunks - 1)
    def _():
        s_out_ref[0] = s_ref[...]


def _retention(q_src, k_src, v_src, sg_src, state, gn_g, bsz, seq, chunk):
    rows = bsz * seq
    n_chunks = seq // chunk
    dqk = N_RET_HEADS * RET_DK
    dv = N_RET_HEADS * RET_DV
    state_blk = (1, N_RET_HEADS, RET_DK, RET_DV)
    est = (2 * 2 * chunk * dqk * 2 + 3 * 2 * chunk * dv * 2 + 4 * _nbytes(state_blk, F32)
           + _nbytes(state_blk, F32) + N_RET_HEADS * chunk * chunk * 4 + 8 * chunk * RET_DV * 4)

    def src_spec(src, width):
        assert src[1] % width == 0
        return pl.BlockSpec((chunk, width), functools.partial(
            lambda b, c, o: (b * n_chunks + c, o), o=src[1] // width))

    return pl.pallas_call(
        functools.partial(_retention_kernel, chunk=chunk, n_chunks=n_chunks),
        grid=(bsz, n_chunks),
        in_specs=[
            src_spec(q_src, dqk), src_spec(k_src, dqk), src_spec(v_src, dv), src_spec(sg_src, dv),
            pl.BlockSpec(state_blk, lambda b, c: (b, 0, 0, 0)),
            pl.BlockSpec((1, dv), lambda b, c: (0, 0)),
        ],
        out_specs=[pl.BlockSpec((chunk, dv), lambda b, c: (b * n_chunks + c, 0)),
                   pl.BlockSpec(state_blk, lambda b, c: (b, 0, 0, 0))],
        out_shape=[jax.ShapeDtypeStruct((rows, dv), BF16),
                   jax.ShapeDtypeStruct((bsz,) + state_blk[1:], F32)],
        scratch_shapes=[pltpu.VMEM(state_blk[1:], F32),
                        pltpu.VMEM((N_RET_HEADS, chunk, chunk), F32)],
        compiler_params=_params(est),
        name="retention",
    )(q_src[0], k_src[0], v_src[0], sg_src[0], state, gn_g.reshape(1, dv))


def _mm_norm_kernel(*refs, n_k, emit_next, sub):
    if emit_next:
        a_ref, w_ref, res_ref, nw_ref, nw2_ref, o_ref, h_ref, acc_ref = refs
    else:
        a_ref, w_ref, res_ref, nw_ref, o_ref, acc_ref = refs
    kk = pl.program_id(1)
    tm = a_ref.shape[0]

    if n_k > 1:
        @pl.when(kk == 0)
        def _():
            acc_ref[...] = _dot(a_ref[...], w_ref[...])
    if n_k > 2:
        @pl.when((kk > 0) & (kk < n_k - 1))
        def _():
            acc_ref[...] += _dot(a_ref[...], w_ref[...])

    @pl.when(kk == n_k - 1)
    def _():
        for s in range(tm // sub):
            rs = slice(s * sub, (s + 1) * sub)
            m = _dot(a_ref[rs, :], w_ref[...])
            if n_k > 1:
                m = m + acc_ref[rs, :]
            inv = lax.rsqrt(jnp.mean(m * m, axis=-1, keepdims=True) + EPS)
            y = res_ref[rs, :] + m * inv * nw_ref[...]
            o_ref[rs, :] = y
            if emit_next:
                inv2 = lax.rsqrt(jnp.mean(y * y, axis=-1, keepdims=True) + EPS)
                h_ref[rs, :] = (y * inv2 * nw2_ref[...]).astype(h_ref.dtype)


def _matmul_norm_residual(a, w, res, norm_w, next_norm_w, tm, tk, name):
    rows, k = a.shape
    d = w.shape[1]
    n_k = k // tk
    emit_next = next_norm_w is not None
    row_i = lambda i, kk: (i, 0)
    const2 = lambda i, kk: (0, 0)
    in_specs = [pl.BlockSpec((tm, tk), lambda i, kk: (i, kk)),
                pl.BlockSpec((tk, d), lambda i, kk: (kk, 0)),
                pl.BlockSpec((tm, d), row_i),
                pl.BlockSpec((1, d), const2)]
    args = [a, w, res, norm_w.reshape(1, d)]
    out_specs = [pl.BlockSpec((tm, d), row_i)]
    out_shape = [jax.ShapeDtypeStruct((rows, d), F32)]
    acc_shape = (tm, d) if n_k > 1 else (SUBLANES, LANES)
    est = (2 * tm * tk * 2 + 2 * tk * d * 2 + 4 * tm * d * 4 + _nbytes(acc_shape, F32)
           + 4 * NORM_SUB_ROWS * d * 4)
    if emit_next:
        in_specs.append(pl.BlockSpec((1, d), const2))
        args.append(next_norm_w.reshape(1, d))
        out_specs.append(pl.BlockSpec((tm, d), row_i))
        out_shape.append(jax.ShapeDtypeStruct((rows, d), BF16))
        est += 2 * tm * d * 2
    outs = pl.pallas_call(
        functools.partial(_mm_norm_kernel, n_k=n_k, emit_next=emit_next, sub=min(tm, NORM_SUB_ROWS)),
        grid=(rows // tm, n_k),
        in_specs=in_specs,
        out_specs=out_specs,
        out_shape=out_shape,
        scratch_shapes=[pltpu.VMEM(acc_shape, F32)],
        compiler_params=_params(est),
        name=name,
    )(*args)
    return outs if emit_next else outs[0]


def _trunk_layer(x, conv_hist, ret_state, start, p):
    bsz, seq, d = x.shape
    rows = bsz * seq
    x2 = x.reshape(rows, d)
    d_qk = N_RET_HEADS * RET_DK
    d_v = N_RET_HEADS * RET_DV
    off_qk = 2 * d
    off_v = off_qk + 2 * d_qk
    off_sg = off_v + d_v
    off_gate = off_sg + d_v

    tm = min(rows, 1024)
    tm_big = 2 * tm if rows % (2 * tm) == 0 else tm
    tn = 1024
    w_in = p["w_in"]

    n_tab = max(seq, tm_big)
    cos_t, sin_t = _rope_tables(start, seq, n_tab, min(n_tab, 1024))

    a, h = _rms_glu(x2, p["norm_mix_pre"], w_in, 0, d, tm, 512)
    if seq % tm_big == 0:
        c_pre, big = _in_proj_conv(h, w_in, a, conv_hist, p["conv_dw"], p["conv_dw_b"],
                                   cos_t, sin_t, p["b_gate"], seq, tm_big)
        q_src, k_src = (big, 0), (big, d_qk)
        v_src, sg_src = (big, 2 * d_qk), (big, 2 * d_qk + d_v)
        gate_c_src, gate_r_src = (big, 2 * d_qk + 2 * d_v), (big, 2 * d_qk + 2 * d_v + d)
    else:
        c_pre = _conv_module(a.reshape(bsz, seq, d), conv_hist, p["conv_dw"], p["conv_dw_b"],
                             min(seq, 128)).reshape(rows, d)
        tab_tiles = n_tab // tm
        tab_spec = pl.BlockSpec((tm, ROPE_HALF), lambda i, j: (i % tab_tiles, 0))
        qk = _matmul(h, [(w_in, off_qk)], [(cos_t, tab_spec), (sin_t, tab_spec)],
                     functools.partial(_epi_rotary, n_q_tiles=d_qk // tn),
                     2 * d_qk, BF16, tm, tn, "in_proj_qk")
        v = _matmul(h, [(w_in, off_v)], [], _epi_identity, d_v, BF16, tm, tn, "in_proj_v")
        sg = _matmul(h, [(w_in, off_sg)], [], _epi_silu, d_v, BF16, tm, tn, "in_proj_swish_gate")
        gates = _matmul(h, [(w_in, off_gate)], [(p["b_gate"].reshape(1, 2 * d), _row_spec(tn, 0))],
                        _epi_sigmoid_bias, 2 * d, BF16, tm, tn, "in_proj_merge_gates")
        q_src, k_src, v_src, sg_src = (qk, 0), (qk, d_qk), (v, 0), (sg, 0)
        gate_c_src, gate_r_src = (gates, 0), (gates, d)

    new_hist = a.reshape(bsz, seq, d)[:, seq - (CONV_WIDTH - 1):, :]
    chunk = min(seq, 256)
    o, s_new = _retention(q_src, k_src, v_src, sg_src, ret_state, p["ret_gn_g"], bsz, seq, chunk)

    tm_n = min(rows, 512)
    y_c = _conv_out_proj(c_pre, p["conv_ln_g"], p["conv_ln_b"], p["w_conv_out"], gate_c_src, tm_n, d)
    mix_in = _matmul(o, [(p["w_ret_out"], 0)],
                     [(gate_r_src[0], _tile_spec(tm, tn, gate_r_src[1])), (y_c, _tile_spec(tm, tn, 0))],
                     _epi_gate_mul_add, d, BF16, tm, tn, "ret_out_proj")

    x1, h2 = _matmul_norm_residual(mix_in, p["w_out"], x2, p["norm_mix_post"], p["norm_ffn_pre"],
                                   tm_n, d, "out_proj_norm")

    d_ff = p["w_ffn_gate"].shape[1]
    f = _matmul(h2, [(p["w_ffn_gate"], 0), (p["w_ffn_up"], 0)], [], _epi_swiglu,
                d_ff, BF16, tm_big, 512, "ffn_up")
    y = _matmul_norm_residual(f, p["w_ffn_down"], x1, p["norm_ffn_post"], None, tm_n, d_ff // 2, "ffn_down_norm")
    return y.reshape(bsz, seq, d), new_hist, s_new


def kernel(x_prompt, x_sample, cache_conv, state_ret, norm_mix_pre, norm_mix_post, w_in, b_gate, conv_dw, conv_dw_b, conv_ln_g, conv_ln_b, w_conv_out, ret_gn_g, w_ret_out, w_out, norm_ffn_pre, norm_ffn_post, w_ffn_gate, w_ffn_up, w_ffn_down):
    depth = w_in.shape[0]
    past_len = 1024
    y_prompt, y_sample = x_prompt, x_sample
    conv_p, ret_p, conv_s, ret_s = [], [], [], []
    n_batch = x_prompt.shape[0]
    for l in range(depth):
        p = {
            "norm_mix_pre": norm_mix_pre[l], "norm_mix_post": norm_mix_post[l],
            "w_in": w_in[l].astype(BF16), "b_gate": b_gate[l],
            "conv_dw": conv_dw[l], "conv_dw_b": conv_dw_b[l],
            "conv_ln_g": conv_ln_g[l], "conv_ln_b": conv_ln_b[l],
            "w_conv_out": w_conv_out[l].astype(BF16), "ret_gn_g": ret_gn_g[l],
            "w_ret_out": w_ret_out[l].astype(BF16), "w_out": w_out[l].astype(BF16),
            "norm_ffn_pre": norm_ffn_pre[l], "norm_ffn_post": norm_ffn_post[l],
            "w_ffn_gate": w_ffn_gate[l].astype(BF16), "w_ffn_up": w_ffn_up[l].astype(BF16),
            "w_ffn_down": w_ffn_down[l].astype(BF16),
        }
        zero_hist = jnp.zeros((n_batch, CONV_WIDTH - 1, x_prompt.shape[-1]), x_prompt.dtype)
        zero_state = jnp.zeros((n_batch, N_RET_HEADS, RET_DK, RET_DV), x_prompt.dtype)
        y_prompt, hp, sp = _trunk_layer(y_prompt, zero_hist, zero_state, 0, p)
        y_sample, hs, ss = _trunk_layer(y_sample, cache_conv[l], state_ret[l], past_len, p)
        conv_p.append(hp)
        ret_p.append(sp)
        conv_s.append(hs)
        ret_s.append(ss)
    return (y_prompt, y_sample, jnp.stack(conv_p), jnp.stack(ret_p), jnp.stack(conv_s), jnp.stack(ret_s))
```

```python
import functools
import math

import jax
import jax.numpy as jnp
from jax import lax
from jax.experimental import pallas as pl
from jax.experimental.pallas import tpu as pltpu

F32 = jnp.float32
BF16 = jnp.bfloat16

EPS = 1e-6
ROPE_BASE = 10000.0
CONV_WIDTH = 31
N_RET_HEADS = 8
RET_DK = 256
RET_DV = 512
ROPE_HALF = RET_DK // 2

MIB = 1024 * 1024
V7X_VMEM_REQUEST_CAP = 56 * MIB
SUBLANES = 8
LANES = 128
CONV_HALO_ROWS = 32
CONV_LEAD = CONV_HALO_ROWS - (CONV_WIDTH - 1)
CONV_ROW_BLK = 64
NORM_SUB_ROWS = 128
MM_SUB_ROWS = 256
IN_PROJ_UNROLL = 4
IN_PROJ_TILE = 1024


def _params(est_bytes):
    limit = min(V7X_VMEM_REQUEST_CAP, max(32 * MIB, int(est_bytes * 1.25)))
    return pltpu.CompilerParams(vmem_limit_bytes=limit)


def _nbytes(shape, dtype):
    return math.prod(shape) * jnp.dtype(dtype).itemsize


def _sigmoid(x):
    return jax.nn.sigmoid(x)


def _silu(x):
    return x * jax.nn.sigmoid(x)


def _dot(a, b):
    return jnp.dot(a, b, preferred_element_type=F32)


def _rms_glu_kernel(x_ref, nw_ref, wv_ref, wg_ref, a_ref, h_ref, *, sub):
    j = pl.program_id(1)
    tm = x_ref.shape[0]

    def glu(rs, h):
        a_ref[rs, :] = _dot(h, wv_ref[...]) * _sigmoid(_dot(h, wg_ref[...]))

    @pl.when(j == 0)
    def _():
        for s in range(tm // sub):
            rs = slice(s * sub, (s + 1) * sub)
            x = x_ref[rs, :]
            ms = jnp.mean(x * x, axis=-1, keepdims=True)
            h = (x * lax.rsqrt(ms + EPS) * nw_ref[...]).astype(h_ref.dtype)
            h_ref[rs, :] = h
            glu(rs, h)

    @pl.when(j > 0)
    def _():
        for s in range(tm // sub):
            rs = slice(s * sub, (s + 1) * sub)
            glu(rs, h_ref[rs, :])


def _rms_glu(x, norm_w, w_in, off_value, off_gate, tm, tn):
    rows, d = x.shape
    est = 2 * tm * d * 4 + 4 * d * tn * 2 + 2 * tm * tn * 4 + 2 * tm * d * 2 + 3 * MM_SUB_ROWS * tn * 4
    w_spec = lambda off: pl.BlockSpec((d, tn), functools.partial(lambda i, j, o: (0, o + j), o=off // tn))
    return pl.pallas_call(
        functools.partial(_rms_glu_kernel, sub=min(tm, MM_SUB_ROWS)),
        grid=(rows // tm, d // tn),
        in_specs=[pl.BlockSpec((tm, d), lambda i, j: (i, 0)),
                  pl.BlockSpec((1, d), lambda i, j: (0, 0)),
                  w_spec(off_value), w_spec(off_gate)],
        out_specs=[pl.BlockSpec((tm, tn), lambda i, j: (i, j)),
                   pl.BlockSpec((tm, d), lambda i, j: (i, 0))],
        out_shape=[jax.ShapeDtypeStruct((rows, d), F32),
                   jax.ShapeDtypeStruct((rows, d), BF16)],
        compiler_params=_params(est),
        name="rms_glu",
    )(x, norm_w.reshape(1, d), w_in, w_in)


def _mm_kernel(*refs, n_w, epilogue, sub):
    a_ref = refs[0]
    w_refs = refs[1:1 + n_w]
    extra_refs = refs[1 + n_w:-1]
    o_ref = refs[-1]
    for s in range(a_ref.shape[0] // sub):
        rs = slice(s * sub, (s + 1) * sub)
        a = a_ref[rs, :]
        accs = [_dot(a, w_ref[...]) for w_ref in w_refs]
        o_ref[rs, :] = epilogue(accs, extra_refs, rs).astype(o_ref.dtype)


def _matmul(a, ws, extras, epilogue, n_out, out_dtype, tm, tn, name):
    rows, k = a.shape
    grid = (rows // tm, n_out // tn)
    in_specs = [pl.BlockSpec((tm, k), lambda i, j: (i, 0))]
    args = [a]
    est = 2 * tm * k * 2
    for w, off in ws:
        assert off % tn == 0
        in_specs.append(pl.BlockSpec((k, tn), functools.partial(lambda i, j, o: (0, o + j), o=off // tn)))
        args.append(w)
        est += 2 * k * tn * 2
    for arr, spec in extras:
        in_specs.append(spec)
        args.append(arr)
        est += 2 * _nbytes(spec.block_shape, arr.dtype)
    est += 2 * tm * tn * jnp.dtype(out_dtype).itemsize
    est += (len(ws) + 1) * tm * tn * 4
    return pl.pallas_call(
        functools.partial(_mm_kernel, n_w=len(ws), epilogue=epilogue, sub=min(tm, MM_SUB_ROWS)),
        grid=grid,
        in_specs=in_specs,
        out_specs=pl.BlockSpec((tm, tn), lambda i, j: (i, j)),
        out_shape=jax.ShapeDtypeStruct((rows, n_out), out_dtype),
        compiler_params=_params(est),
        name=name,
    )(*args)


def _tile_spec(tm, tn, off):
    assert off % tn == 0
    return pl.BlockSpec((tm, tn), functools.partial(lambda i, j, o: (i, o + j), o=off // tn))


def _row_spec(tn, off):
    return pl.BlockSpec((1, tn), functools.partial(lambda i, j, o: (0, o + j), o=off // tn))


def _rotary(acc, cos, sin, scale):
    outs = []
    for h in range(acc.shape[-1] // RET_DK):
        x1 = acc[:, h * RET_DK:h * RET_DK + ROPE_HALF]
        x2 = acc[:, h * RET_DK + ROPE_HALF:(h + 1) * RET_DK]
        outs.append((x1 * cos - x2 * sin) * scale)
        outs.append((x1 * sin + x2 * cos) * scale)
    return jnp.concatenate(outs, axis=-1)


def _epi_identity(accs, extras, rs):
    return accs[0]


def _epi_silu(accs, extras, rs):
    return _silu(accs[0])


def _epi_sigmoid_bias(accs, extras, rs):
    return _sigmoid(accs[0] + extras[0][...])


def _epi_rotary(accs, extras, rs, *, n_q_tiles):
    scale = jnp.where(pl.program_id(1) >= n_q_tiles, RET_DK ** -0.5, 1.0).astype(F32)
    return _rotary(accs[0], extras[0][rs, :], extras[1][rs, :], scale)


def _epi_gate_mul_add(accs, extras, rs):
    return extras[0][rs, :].astype(F32) * accs[0] + extras[1][rs, :]


def _epi_swiglu(accs, extras, rs):
    return _silu(accs[0]) * accs[1]


def _rope_kernel(inv_ref, cos_ref, sin_ref, *, start, seq_len, tt):
    row = pl.program_id(0) * tt + lax.broadcasted_iota(jnp.int32, (tt, ROPE_HALF), 0)
    pos = (start + lax.rem(row, seq_len)).astype(F32)
    ang = pos * inv_ref[...]
    cos_ref[...] = jnp.cos(ang)
    sin_ref[...] = jnp.sin(ang)


def _rope_tables(start, seq_len, n_rows, tt):
    inv_freq = 1.0 / (ROPE_BASE ** (jnp.arange(ROPE_HALF, dtype=F32) / ROPE_HALF))
    shape = jax.ShapeDtypeStruct((n_rows, ROPE_HALF), F32)
    return pl.pallas_call(
        functools.partial(_rope_kernel, start=start, seq_len=seq_len, tt=tt),
        grid=(n_rows // tt,),
        in_specs=[pl.BlockSpec((1, ROPE_HALF), lambda i: (0, 0))],
        out_specs=[pl.BlockSpec((tt, ROPE_HALF), lambda i: (i, 0))] * 2,
        out_shape=[shape, shape],
        name="rope_tables",
    )(inv_freq.reshape(1, ROPE_HALF))


def _conv_taps(x, w_row, bias, n_rows):
    acc = None
    for r in range(SUBLANES):
        rows = n_rows if r == 0 else n_rows + SUBLANES
        g = None
        for p in range((CONV_WIDTH + CONV_LEAD + SUBLANES - 1) // SUBLANES):
            j = SUBLANES * p + r - CONV_LEAD
            if 0 <= j < CONV_WIDTH:
                term = w_row(j) * x[SUBLANES * p:SUBLANES * p + rows]
                g = term if g is None else g + term
        shifted = g[r:r + n_rows]
        acc = shifted if acc is None else acc + shifted
    return acc + bias


def _conv_kernel(cur_ref, prev_ref, hist_ref, w_ref, b_ref, o_ref, win_ref, *, tt, row_blk):
    t = pl.program_id(1)

    @pl.when(t == 0)
    def _():
        win_ref[CONV_LEAD:CONV_HALO_ROWS, :] = hist_ref[0]

    @pl.when(t > 0)
    def _():
        win_ref[0:CONV_HALO_ROWS, :] = prev_ref[0]

    win_ref[CONV_HALO_ROWS:CONV_HALO_ROWS + tt, :] = cur_ref[0]

    d = cur_ref.shape[-1]
    for cb in range(d // LANES):
        cs = slice(cb * LANES, (cb + 1) * LANES)
        for rb in range(tt // row_blk):
            x = win_ref[rb * row_blk:rb * row_blk + row_blk + CONV_HALO_ROWS, cs]
            o_ref[0, rb * row_blk:(rb + 1) * row_blk, cs] = _conv_taps(
                x, lambda j: w_ref[j:j + 1, cs], b_ref[:, cs], row_blk)


def _conv_module(a, hist, conv_dw, conv_dw_b, tt):
    bsz, seq, d = a.shape
    n_hist = CONV_WIDTH - 1
    halo_per_tile = tt // CONV_HALO_ROWS
    row_blk = min(tt, CONV_ROW_BLK)
    est = 2 * tt * d * 4 + 4 * CONV_HALO_ROWS * d * 4 + 2 * tt * d * 4 + (tt + CONV_HALO_ROWS) * d * 4
    const2 = lambda b, t: (0, 0)
    return pl.pallas_call(
        functools.partial(_conv_kernel, tt=tt, row_blk=row_blk),
        grid=(bsz, seq // tt),
        in_specs=[
            pl.BlockSpec((1, tt, d), lambda b, t: (b, t, 0)),
            pl.BlockSpec((1, CONV_HALO_ROWS, d),
                         lambda b, t: (b, jnp.maximum(t * halo_per_tile - 1, 0), 0)),
            pl.BlockSpec((1, n_hist, d), lambda b, t: (b, 0, 0)),
            pl.BlockSpec((CONV_WIDTH, d), const2),
            pl.BlockSpec((1, d), const2),
        ],
        out_specs=pl.BlockSpec((1, tt, d), lambda b, t: (b, t, 0)),
        out_shape=jax.ShapeDtypeStruct((bsz, seq, d), F32),
        scratch_shapes=[pltpu.VMEM((CONV_HALO_ROWS + tt, d), F32)],
        compiler_params=_params(est),
        name="conv_module",
    )(a, a, hist, conv_dw, conv_dw_b.reshape(1, d))


def _in_proj_conv_kernel(h_ref, w_ref, acur_ref, aprev_ref, hist_ref, cw_ref, cb_ref, cos_ref, sin_ref,
                         bg_ref, cpre_ref, big_ref, win_ref, *, tm, tiles_per_seq, bounds):
    i = pl.program_id(0)
    t = pl.program_id(1)
    k_start, v_start, sg_start, gate_start = bounds
    n_sub = tm // MM_SUB_ROWS
    conv_blks_per_sub = (tm // CONV_ROW_BLK) // n_sub

    @pl.when(i % tiles_per_seq == 0)
    def _():
        win_ref[CONV_LEAD:CONV_HALO_ROWS, :] = hist_ref[0, t]

    @pl.when(i % tiles_per_seq != 0)
    def _():
        win_ref[0:CONV_HALO_ROWS, :] = aprev_ref[...]

    win_ref[CONV_HALO_ROWS:CONV_HALO_ROWS + tm, :] = acur_ref[...]

    def step(epilogue):
        half = w_ref.shape[-1] // 2
        conv_blks_per_half = conv_blks_per_sub // 2

        def sub_tiles(it, carry):
            for u in range(IN_PROJ_UNROLL):
                r_sub = pl.multiple_of((it * IN_PROJ_UNROLL + u) * MM_SUB_ROWS, MM_SUB_ROWS)
                rs = pl.ds(r_sub, MM_SUB_ROWS)
                for c in range(2):
                    cs = slice(c * half, (c + 1) * half)
                    acc = _dot(h_ref[rs, :], w_ref[:, cs])
                    for b in range(conv_blks_per_half):
                        r0 = pl.multiple_of(r_sub + (c * conv_blks_per_half + b) * CONV_ROW_BLK, CONV_ROW_BLK)
                        x = win_ref[pl.ds(r0, CONV_ROW_BLK + CONV_HALO_ROWS), :]
                        cpre_ref[pl.ds(r0, CONV_ROW_BLK), :] = _conv_taps(
                            x, lambda j: cw_ref[t, j:j + 1, :], cb_ref[t], CONV_ROW_BLK)
                    big_ref[rs, cs] = epilogue(acc, rs, cs).astype(big_ref.dtype)
            return carry

        lax.fori_loop(0, n_sub // IN_PROJ_UNROLL, sub_tiles, 0)

    def epi_rotary(acc, rs, cs):
        scale = jnp.where(t >= k_start, RET_DK ** -0.5, 1.0).astype(F32)
        return _rotary(acc, cos_ref[rs, :], sin_ref[rs, :], scale)

    pl.when(t < v_start)(lambda: step(epi_rotary))
    pl.when((t >= v_start) & (t < sg_start))(lambda: step(lambda acc, rs, cs: acc))
    pl.when((t >= sg_start) & (t < gate_start))(lambda: step(lambda acc, rs, cs: _silu(acc)))
    pl.when(t >= gate_start)(lambda: step(lambda acc, rs, cs: _sigmoid(acc + bg_ref[t - gate_start, :, cs])))


def _in_proj_conv(h, w_in, a, hist, conv_dw, conv_dw_b, cos_t, sin_t, b_gate, seq, tm):
    rows, d = h.shape
    tn = IN_PROJ_TILE
    n_chunks = d // LANES
    d_qk = N_RET_HEADS * RET_DK
    d_v = N_RET_HEADS * RET_DV
    off_wide = 2 * d
    wide_cols = 2 * d_qk + 2 * d_v + 2 * d
    assert wide_cols // tn == n_chunks and seq % tm == 0 and tm % CONV_HALO_ROWS == 0
    tiles_per_seq = seq // tm
    tab_tiles = cos_t.shape[0] // tm
    halo_per_tile = tm // CONV_HALO_ROWS
    bounds = (d_qk // tn, 2 * d_qk // tn, (2 * d_qk + d_v) // tn, (2 * d_qk + 2 * d_v) // tn)
    n_bias_steps = 2 * d // tn
    whole3 = lambda i, t: (0, 0, 0)
    tab_map = lambda i, t: (i % tab_tiles, 0)
    cw = conv_dw.reshape(CONV_WIDTH, n_chunks, LANES).transpose(1, 0, 2)
    cb = conv_dw_b.reshape(n_chunks, 1, LANES)
    hist_cm = hist.reshape(hist.shape[0], CONV_WIDTH - 1, n_chunks, LANES).transpose(0, 2, 1, 3)
    est = (2 * tm * d * 2 + 2 * d * tn * 2 + 4 * tm * LANES * 4 + 4 * tm * ROPE_HALF * 4
           + 2 * tm * tn * 2 + (tm + CONV_HALO_ROWS) * LANES * 4 + 3 * MM_SUB_ROWS * tn * 4)
    return pl.pallas_call(
        functools.partial(_in_proj_conv_kernel, tm=tm, tiles_per_seq=tiles_per_seq, bounds=bounds),
        grid=(rows // tm, n_chunks),
        in_specs=[
            pl.BlockSpec((tm, d), lambda i, t: (i, 0)),
            pl.BlockSpec((d, tn), lambda i, t: (0, off_wide // tn + t)),
            pl.BlockSpec((tm, LANES), lambda i, t: (i, t)),
            pl.BlockSpec((CONV_HALO_ROWS, LANES), lambda i, t: (jnp.maximum(i * halo_per_tile - 1, 0), t)),
            pl.BlockSpec((1, n_chunks, CONV_WIDTH - 1, LANES), lambda i, t: (i // tiles_per_seq, 0, 0, 0)),
            pl.BlockSpec((n_chunks, CONV_WIDTH, LANES), whole3),
            pl.BlockSpec((n_chunks, 1, LANES), whole3),
            pl.BlockSpec((tm, ROPE_HALF), tab_map),
            pl.BlockSpec((tm, ROPE_HALF), tab_map),
            pl.BlockSpec((n_bias_steps, 1, tn), whole3),
        ],
        out_specs=[
            pl.BlockSpec((tm, LANES), lambda i, t: (i, t)),
            pl.BlockSpec((tm, tn), lambda i, t: (i, t)),
        ],
        out_shape=[jax.ShapeDtypeStruct((rows, d), F32),
                   jax.ShapeDtypeStruct((rows, wide_cols), BF16)],
        scratch_shapes=[pltpu.VMEM((CONV_HALO_ROWS + tm, LANES), F32)],
        compiler_params=_params(est),
        name="in_proj_conv",
    )(h, w_in, a, a, hist_cm, cw, cb, cos_t, sin_t, b_gate.reshape(n_bias_steps, 1, tn))


def _ln_mm_kernel(c_ref, g_ref, b_ref, w_ref, gate_ref, o_ref, c_scr, *, sub):
    j = pl.program_id(1)
    tm = c_ref.shape[0]

    @pl.when(j == 0)
    def _():
        for s in range(tm // sub):
            rs = slice(s * sub, (s + 1) * sub)
            c = c_ref[rs, :]
            mu = jnp.mean(c, axis=-1, keepdims=True)
            cc = c - mu
            var = jnp.mean(cc * cc, axis=-1, keepdims=True)
            y = cc * lax.rsqrt(var + EPS) * g_ref[...] + b_ref[...]
            act = _silu(y).astype(c_scr.dtype)
            c_scr[rs, :] = act
            o_ref[rs, :] = gate_ref[rs, :].astype(F32) * _dot(act, w_ref[...])

    @pl.when(j > 0)
    def _():
        o_ref[...] = gate_ref[...].astype(F32) * _dot(c_scr[...], w_ref[...])


def _conv_out_proj(c_pre, ln_g, ln_b, w, gate_src, tm, tn):
    rows, d = c_pre.shape
    n_out = w.shape[1]
    gates, gate_off = gate_src
    est = 2 * tm * d * 4 + 2 * d * tn * 2 + 2 * tm * tn * 2 + 2 * tm * tn * 4 + tm * d * 2 + 6 * NORM_SUB_ROWS * d * 4
    const2 = lambda i, j: (0, 0)
    return pl.pallas_call(
        functools.partial(_ln_mm_kernel, sub=min(tm, NORM_SUB_ROWS)),
        grid=(rows // tm, n_out // tn),
        in_specs=[pl.BlockSpec((tm, d), lambda i, j: (i, 0)),
                  pl.BlockSpec((1, d), const2),
                  pl.BlockSpec((1, d), const2),
                  pl.BlockSpec((d, tn), lambda i, j: (0, j)),
                  _tile_spec(tm, tn, gate_off)],
        out_specs=pl.BlockSpec((tm, tn), lambda i, j: (i, j)),
        out_shape=jax.ShapeDtypeStruct((rows, n_out), F32),
        scratch_shapes=[pltpu.VMEM((tm, d), BF16)],
        compiler_params=_params(est),
        name="conv_out_proj",
    )(c_pre, ln_g.reshape(1, d), ln_b.reshape(1, d), w, gates)


def _log_decay(h):
    return math.log1p(-(2.0 ** (-5.0 - h)))


def _retention_kernel(q_ref, k_ref, v_ref, sg_ref, s0_ref, gn_ref, o_ref, s_out_ref,
                      s_ref, decay_ref, *, chunk, n_chunks):
    b = pl.program_id(0)
    c = pl.program_id(1)

    @pl.when((b == 0) & (c == 0))
    def _():
        n = lax.broadcasted_iota(jnp.int32, (chunk, chunk), 0)
        m = lax.broadcasted_iota(jnp.int32, (chunk, chunk), 1)
        diff = (n - m).astype(F32)
        for h in range(N_RET_HEADS):
            decay_ref[h] = jnp.where(diff >= 0.0, jnp.exp(jnp.maximum(diff, 0.0) * _log_decay(h)), 0.0)

    @pl.when(c == 0)
    def _():
        s_ref[...] = s0_ref[0]

    idx = lax.broadcasted_iota(jnp.int32, (chunk, 1), 0).astype(F32)
    for h in range(N_RET_HEADS):
        lg = _log_decay(h)
        q = q_ref[:, h * RET_DK:(h + 1) * RET_DK]
        k = k_ref[:, h * RET_DK:(h + 1) * RET_DK]
        v = v_ref[:, h * RET_DV:(h + 1) * RET_DV]
        s_old = s_ref[h]
        scores = lax.dot_general(q, k, (((1,), (1,)), ((), ())), preferred_element_type=F32)
        scores = (scores * decay_ref[h]).astype(BF16)
        o = _dot(scores, v)
        xi = jnp.exp((idx + 1.0) * lg)
        o = o + _dot(q, s_old.astype(BF16)) * xi
        zeta = jnp.exp((chunk - 1.0 - idx) * lg)
        kz = (k.astype(F32) * zeta).astype(BF16)
        s_ref[h] = math.exp(chunk * lg) * s_old + lax.dot_general(
            kz, v, (((0,), (0,)), ((), ())), preferred_element_type=F32)
        mu = jnp.mean(o, axis=-1, keepdims=True)
        oc = o - mu
        var = jnp.mean(oc * oc, axis=-1, keepdims=True)
        on = oc * lax.rsqrt(var + EPS) * gn_ref[:, h * RET_DV:(h + 1) * RET_DV]
        gate = sg_ref[:, h * RET_DV:(h + 1) * RET_DV].astype(F32)
        o_ref[:, h * RET_DV:(h + 1) * RET_DV] = (on * gate).astype(o_ref.dtype)

    @pl.when(c == n_chunks - 1)
    def _():
        s_out_ref[0] = s_ref[...]


def _retention(q_src, k_src, v_src, sg_src, state, gn_g, bsz, seq, chunk):
    rows = bsz * seq
    n_chunks = seq // chunk
    dqk = N_RET_HEADS * RET_DK
    dv = N_RET_HEADS * RET_DV
    state_blk = (1, N_RET_HEADS, RET_DK, RET_DV)
    est = (2 * 2 * chunk * dqk * 2 + 3 * 2 * chunk * dv * 2 + 4 * _nbytes(state_blk, F32)
           + _nbytes(state_blk, F32) + N_RET_HEADS * chunk * chunk * 4 + 8 * chunk * RET_DV * 4)

    def src_spec(src, width):
        assert src[1] % width == 0
        return pl.BlockSpec((chunk, width), functools.partial(
            lambda b, c, o: (b * n_chunks + c, o), o=src[1] // width))

    return pl.pallas_call(
        functools.partial(_retention_kernel, chunk=chunk, n_chunks=n_chunks),
        grid=(bsz, n_chunks),
        in_specs=[
            src_spec(q_src, dqk), src_spec(k_src, dqk), src_spec(v_src, dv), src_spec(sg_src, dv),
            pl.BlockSpec(state_blk, lambda b, c: (b, 0, 0, 0)),
            pl.BlockSpec((1, dv), lambda b, c: (0, 0)),
        ],
        out_specs=[pl.BlockSpec((chunk, dv), lambda b, c: (b * n_chunks + c, 0)),
                   pl.BlockSpec(state_blk, lambda b, c: (b, 0, 0, 0))],
        out_shape=[jax.ShapeDtypeStruct((rows, dv), BF16),
                   jax.ShapeDtypeStruct((bsz,) + state_blk[1:], F32)],
        scratch_shapes=[pltpu.VMEM(state_blk[1:], F32),
                        pltpu.VMEM((N_RET_HEADS, chunk, chunk), F32)],
        compiler_params=_params(est),
        name="retention",
    )(q_src[0], k_src[0], v_src[0], sg_src[0], state, gn_g.reshape(1, dv))


def _mm_norm_kernel(*refs, n_k, emit_next, sub):
    if emit_next:
        a_ref, w_ref, res_ref, nw_ref, nw2_ref, o_ref, h_ref, acc_ref = refs
    else:
        a_ref, w_ref, res_ref, nw_ref, o_ref, acc_ref = refs
    kk = pl.program_id(1)
    tm = a_ref.shape[0]

    if n_k > 1:
        @pl.when(kk == 0)
        def _():
            acc_ref[...] = _dot(a_ref[...], w_ref[...])
    if n_k > 2:
        @pl.when((kk > 0) & (kk < n_k - 1))
        def _():
            acc_ref[...] += _dot(a_ref[...], w_ref[...])

    @pl.when(kk == n_k - 1)
    def _():
        for s in range(tm // sub):
            rs = slice(s * sub, (s + 1) * sub)
            m = _dot(a_ref[rs, :], w_ref[...])
            if n_k > 1:
                m = m + acc_ref[rs, :]
            inv = lax.rsqrt(jnp.mean(m * m, axis=-1, keepdims=True) + EPS)
            y = res_ref[rs, :] + m * inv * nw_ref[...]
            o_ref[rs, :] = y
            if emit_next:
                inv2 = lax.rsqrt(jnp.mean(y * y, axis=-1, keepdims=True) + EPS)
                h_ref[rs, :] = (y * inv2 * nw2_ref[...]).astype(h_ref.dtype)


def _matmul_norm_residual(a, w, res, norm_w, next_norm_w, tm, tk, name):
    rows, k = a.shape
    d = w.shape[1]
    n_k = k // tk
    emit_next = next_norm_w is not None
    row_i = lambda i, kk: (i, 0)
    const2 = lambda i, kk: (0, 0)
    in_specs = [pl.BlockSpec((tm, tk), lambda i, kk: (i, kk)),
                pl.BlockSpec((tk, d), lambda i, kk: (kk, 0)),
                pl.BlockSpec((tm, d), row_i),
                pl.BlockSpec((1, d), const2)]
    args = [a, w, res, norm_w.reshape(1, d)]
    out_specs = [pl.BlockSpec((tm, d), row_i)]
    out_shape = [jax.ShapeDtypeStruct((rows, d), F32)]
    acc_shape = (tm, d) if n_k > 1 else (SUBLANES, LANES)
    est = (2 * tm * tk * 2 + 2 * tk * d * 2 + 4 * tm * d * 4 + _nbytes(acc_shape, F32)
           + 4 * NORM_SUB_ROWS * d * 4)
    if emit_next:
        in_specs.append(pl.BlockSpec((1, d), const2))
        args.append(next_norm_w.reshape(1, d))
        out_specs.append(pl.BlockSpec((tm, d), row_i))
        out_shape.append(jax.ShapeDtypeStruct((rows, d), BF16))
        est += 2 * tm * d * 2
    outs = pl.pallas_call(
        functools.partial(_mm_norm_kernel, n_k=n_k, emit_next=emit_next, sub=min(tm, NORM_SUB_ROWS)),
        grid=(rows // tm, n_k),
        in_specs=in_specs,
        out_specs=out_specs,
        out_shape=out_shape,
        scratch_shapes=[pltpu.VMEM(acc_shape, F32)],
        compiler_params=_params(est),
        name=name,
    )(*args)
    return outs if emit_next else outs[0]


def _trunk_layer(x, conv_hist, ret_state, start, p):
    bsz, seq, d = x.shape
    rows = bsz * seq
    x2 = x.reshape(rows, d)
    d_qk = N_RET_HEADS * RET_DK
    d_v = N_RET_HEADS * RET_DV
    off_qk = 2 * d
    off_v = off_qk + 2 * d_qk
    off_sg = off_v + d_v
    off_gate = off_sg + d_v

    tm = min(rows, 1024)
    tm_big = 2 * tm if rows % (2 * tm) == 0 else tm
    tn = 1024
    w_in = p["w_in"]

    n_tab = max(seq, tm_big)
    cos_t, sin_t = _rope_tables(start, seq, n_tab, min(n_tab, 1024))

    a, h = _rms_glu(x2, p["norm_mix_pre"], w_in, 0, d, tm, 512)
    if seq % tm_big == 0:
        c_pre, big = _in_proj_conv(h, w_in, a, conv_hist, p["conv_dw"], p["conv_dw_b"],
                                   cos_t, sin_t, p["b_gate"], seq, tm_big)
        q_src, k_src = (big, 0), (big, d_qk)
        v_src, sg_src = (big, 2 * d_qk), (big, 2 * d_qk + d_v)
        gate_c_src, gate_r_src = (big, 2 * d_qk + 2 * d_v), (big, 2 * d_qk + 2 * d_v + d)
    else:
        c_pre = _conv_module(a.reshape(bsz, seq, d), conv_hist, p["conv_dw"], p["conv_dw_b"],
                             min(seq, 128)).reshape(rows, d)
        tab_tiles = n_tab // tm
        tab_spec = pl.BlockSpec((tm, ROPE_HALF), lambda i, j: (i % tab_tiles, 0))
        qk = _matmul(h, [(w_in, off_qk)], [(cos_t, tab_spec), (sin_t, tab_spec)],
                     functools.partial(_epi_rotary, n_q_tiles=d_qk // tn),
                     2 * d_qk, BF16, tm, tn, "in_proj_qk")
        v = _matmul(h, [(w_in, off_v)], [], _epi_identity, d_v, BF16, tm, tn, "in_proj_v")
        sg = _matmul(h, [(w_in, off_sg)], [], _epi_silu, d_v, BF16, tm, tn, "in_proj_swish_gate")
        gates = _matmul(h, [(w_in, off_gate)], [(p["b_gate"].reshape(1, 2 * d), _row_spec(tn, 0))],
                        _epi_sigmoid_bias, 2 * d, BF16, tm, tn, "in_proj_merge_gates")
        q_src, k_src, v_src, sg_src = (qk, 0), (qk, d_qk), (v, 0), (sg, 0)
        gate_c_src, gate_r_src = (gates, 0), (gates, d)

    new_hist = a.reshape(bsz, seq, d)[:, seq - (CONV_WIDTH - 1):, :]
    chunk = min(seq, 256)
    o, s_new = _retention(q_src, k_src, v_src, sg_src, ret_state, p["ret_gn_g"], bsz, seq, chunk)

    tm_n = min(rows, 512)
    y_c = _conv_out_proj(c_pre, p["conv_ln_g"], p["conv_ln_b"], p["w_conv_out"], gate_c_src, tm_n, d)
    mix_in = _matmul(o, [(p["w_ret_out"], 0)],
                     [(gate_r_src[0], _tile_spec(tm, tn, gate_r_src[1])), (y_c, _tile_spec(tm, tn, 0))],
                     _epi_gate_mul_add, d, BF16, tm, tn, "ret_out_proj")

    x1, h2 = _matmul_norm_residual(mix_in, p["w_out"], x2, p["norm_mix_post"], p["norm_ffn_pre"],
                                   tm_n, d, "out_proj_norm")

    d_ff = p["w_ffn_gate"].shape[1]
    f = _matmul(h2, [(p["w_ffn_gate"], 0), (p["w_ffn_up"], 0)], [], _epi_swiglu,
                d_ff, BF16, tm_big, 512, "ffn_up")
    y = _matmul_norm_residual(f, p["w_ffn_down"], x1, p["norm_ffn_post"], None, tm_n, d_ff // 2, "ffn_down_norm")
    return y.reshape(bsz, seq, d), new_hist, s_new


def kernel(x_prompt, x_sample, cache_conv, state_ret, norm_mix_pre, norm_mix_post, w_in, b_gate, conv_dw, conv_dw_b, conv_ln_g, conv_ln_b, w_conv_out, ret_gn_g, w_ret_out, w_out, norm_ffn_pre, norm_ffn_post, w_ffn_gate, w_ffn_up, w_ffn_down):
    depth = w_in.shape[0]
    past_len = 1024
    y_prompt, y_sample = x_prompt, x_sample
    conv_p, ret_p, conv_s, ret_s = [], [], [], []
    n_batch = x_prompt.shape[0]
    for l in range(depth):
        p = {
            "norm_mix_pre": norm_mix_pre[l], "norm_mix_post": norm_mix_post[l],
            "w_in": w_in[l].astype(BF16), "b_gate": b_gate[l],
            "conv_dw": conv_dw[l], "conv_dw_b": conv_dw_b[l],
            "conv_ln_g": conv_ln_g[l], "conv_ln_b": conv_ln_b[l],
            "w_conv_out": w_conv_out[l].astype(BF16), "ret_gn_g": ret_gn_g[l],
            "w_ret_out": w_ret_out[l].astype(BF16), "w_out": w_out[l].astype(BF16),
            "norm_ffn_pre": norm_ffn_pre[l], "norm_ffn_post": norm_ffn_post[l],
            "w_ffn_gate": w_ffn_gate[l].astype(BF16), "w_ffn_up": w_ffn_up[l].astype(BF16),
            "w_ffn_down": w_ffn_down[l].astype(BF16),
        }
        zero_hist = jnp.zeros((n_batch, CONV_WIDTH - 1, x_prompt.shape[-1]), x_prompt.dtype)
        zero_state = jnp.zeros((n_batch, N_RET_HEADS, RET_DK, RET_DV), x_prompt.dtype)
        y_prompt, hp, sp = _trunk_layer(y_prompt, zero_hist, zero_state, 0, p)
        y_sample, hs, ss = _trunk_layer(y_sample, cache_conv[l], state_ret[l], past_len, p)
        conv_p.append(hp)
        ret_p.append(sp)
        conv_s.append(hs)
        ret_s.append(ss)
    return (y_prompt, y_sample, jnp.stack(conv_p), jnp.stack(ret_p), jnp.stack(conv_s), jnp.stack(ret_s))
```

```python
import functools
import math

import jax
import jax.numpy as jnp
from jax import lax
from jax.experimental import pallas as pl
from jax.experimental.pallas import tpu as pltpu

F32 = jnp.float32
BF16 = jnp.bfloat16

EPS = 1e-6
ROPE_BASE = 10000.0
CONV_WIDTH = 31
N_RET_HEADS = 8
RET_DK = 256
RET_DV = 512
ROPE_HALF = RET_DK // 2

MIB = 1024 * 1024
V7X_VMEM_REQUEST_CAP = 56 * MIB
SUBLANES = 8
LANES = 128
CONV_HALO_ROWS = 32
CONV_LEAD = CONV_HALO_ROWS - (CONV_WIDTH - 1)
CONV_ROW_BLK = 64
NORM_SUB_ROWS = 128
MM_SUB_ROWS = 256
IN_PROJ_UNROLL = 4
IN_PROJ_TILE = 1024


def _params(est_bytes):
    limit = min(V7X_VMEM_REQUEST_CAP, max(32 * MIB, int(est_bytes * 1.25)))
    return pltpu.CompilerParams(vmem_limit_bytes=limit)


def _nbytes(shape, dtype):
    return math.prod(shape) * jnp.dtype(dtype).itemsize


def _sigmoid(x):
    return jax.nn.sigmoid(x)


def _silu(x):
    return x * jax.nn.sigmoid(x)


def _dot(a, b):
    return jnp.dot(a, b, preferred_element_type=F32)


def _rms_glu_kernel(x_ref, nw_ref, wv_ref, wg_ref, a_ref, h_ref, *, sub):
    j = pl.program_id(1)
    tm = x_ref.shape[0]

    def glu(rs, h):
        a_ref[rs, :] = _dot(h, wv_ref[...]) * _sigmoid(_dot(h, wg_ref[...]))

    @pl.when(j == 0)
    def _():
        for s in range(tm // sub):
            rs = slice(s * sub, (s + 1) * sub)
            x = x_ref[rs, :]
            ms = jnp.mean(x * x, axis=-1, keepdims=True)
            h = (x * lax.rsqrt(ms + EPS) * nw_ref[...]).astype(h_ref.dtype)
            h_ref[rs, :] = h
            glu(rs, h)

    @pl.when(j > 0)
    def _():
        for s in range(tm // sub):
            rs = slice(s * sub, (s + 1) * sub)
            glu(rs, h_ref[rs, :])


def _rms_glu(x, norm_w, w_in, off_value, off_gate, tm, tn):
    rows, d = x.shape
    est = 2 * tm * d * 4 + 4 * d * tn * 2 + 2 * tm * tn * 4 + 2 * tm * d * 2 + 3 * MM_SUB_ROWS * tn * 4
    w_spec = lambda off: pl.BlockSpec((d, tn), functools.partial(lambda i, j, o: (0, o + j), o=off // tn))
    return pl.pallas_call(
        functools.partial(_rms_glu_kernel, sub=min(tm, MM_SUB_ROWS)),
        grid=(rows // tm, d // tn),
        in_specs=[pl.BlockSpec((tm, d), lambda i, j: (i, 0)),
                  pl.BlockSpec((1, d), lambda i, j: (0, 0)),
                  w_spec(off_value), w_spec(off_gate)],
        out_specs=[pl.BlockSpec((tm, tn), lambda i, j: (i, j)),
                   pl.BlockSpec((tm, d), lambda i, j: (i, 0))],
        out_shape=[jax.ShapeDtypeStruct((rows, d), F32),
                   jax.ShapeDtypeStruct((rows, d), BF16)],
        compiler_params=_params(est),
        name="rms_glu",
    )(x, norm_w.reshape(1, d), w_in, w_in)


def _mm_kernel(*refs, n_w, epilogue, sub):
    a_ref = refs[0]
    w_refs = refs[1:1 + n_w]
    extra_refs = refs[1 + n_w:-1]
    o_ref = refs[-1]
    for s in range(a_ref.shape[0] // sub):
        rs = slice(s * sub, (s + 1) * sub)
        a = a_ref[rs, :]
        accs = [_dot(a, w_ref[...]) for w_ref in w_refs]
        o_ref[rs, :] = epilogue(accs, extra_refs, rs).astype(o_ref.dtype)


def _matmul(a, ws, extras, epilogue, n_out, out_dtype, tm, tn, name):
    rows, k = a.shape
    grid = (rows // tm, n_out // tn)
    in_specs = [pl.BlockSpec((tm, k), lambda i, j: (i, 0))]
    args = [a]
    est = 2 * tm * k * 2
    for w, off in ws:
        assert off % tn == 0
        in_specs.append(pl.BlockSpec((k, tn), functools.partial(lambda i, j, o: (0, o + j), o=off // tn)))
        args.append(w)
        est += 2 * k * tn * 2
    for arr, spec in extras:
        in_specs.append(spec)
        args.append(arr)
        est += 2 * _nbytes(spec.block_shape, arr.dtype)
    est += 2 * tm * tn * jnp.dtype(out_dtype).itemsize
    est += (len(ws) + 1) * tm * tn * 4
    return pl.pallas_call(
        functools.partial(_mm_kernel, n_w=len(ws), epilogue=epilogue, sub=min(tm, MM_SUB_ROWS)),
        grid=grid,
        in_specs=in_specs,
        out_specs=pl.BlockSpec((tm, tn), lambda i, j: (i, j)),
        out_shape=jax.ShapeDtypeStruct((rows, n_out), out_dtype),
        compiler_params=_params(est),
        name=name,
    )(*args)


def _tile_spec(tm, tn, off):
    assert off % tn == 0
    return pl.BlockSpec((tm, tn), functools.partial(lambda i, j, o: (i, o + j), o=off // tn))


def _row_spec(tn, off):
    return pl.BlockSpec((1, tn), functools.partial(lambda i, j, o: (0, o + j), o=off // tn))


def _rotary(acc, cos, sin, scale):
    outs = []
    for h in range(acc.shape[-1] // RET_DK):
        x1 = acc[:, h * RET_DK:h * RET_DK + ROPE_HALF]
        x2 = acc[:, h * RET_DK + ROPE_HALF:(h + 1) * RET_DK]
        outs.append((x1 * cos - x2 * sin) * scale)
        outs.append((x1 * sin + x2 * cos) * scale)
    return jnp.concatenate(outs, axis=-1)


def _epi_identity(accs, extras, rs):
    return accs[0]


def _epi_silu(accs, extras, rs):
    return _silu(accs[0])


def _epi_sigmoid_bias(accs, extras, rs):
    return _sigmoid(accs[0] + extras[0][...])


def _epi_rotary(accs, extras, rs, *, n_q_tiles):
    scale = jnp.where(pl.program_id(1) >= n_q_tiles, RET_DK ** -0.5, 1.0).astype(F32)
    return _rotary(accs[0], extras[0][rs, :], extras[1][rs, :], scale)


def _epi_gate_mul_add(accs, extras, rs):
    return extras[0][rs, :].astype(F32) * accs[0] + extras[1][rs, :]


def _epi_swiglu(accs, extras, rs):
    return _silu(accs[0]) * accs[1]


def _rope_kernel(inv_ref, cos_ref, sin_ref, *, start, seq_len, tt):
    row = pl.program_id(0) * tt + lax.broadcasted_iota(jnp.int32, (tt, ROPE_HALF), 0)
    pos = (start + lax.rem(row, seq_len)).astype(F32)
    ang = pos * inv_ref[...]
    cos_ref[...] = jnp.cos(ang)
    sin_ref[...] = jnp.sin(ang)


def _rope_tables(start, seq_len, n_rows, tt):
    inv_freq = 1.0 / (ROPE_BASE ** (jnp.arange(ROPE_HALF, dtype=F32) / ROPE_HALF))
    shape = jax.ShapeDtypeStruct((n_rows, ROPE_HALF), F32)
    return pl.pallas_call(
        functools.partial(_rope_kernel, start=start, seq_len=seq_len, tt=tt),
        grid=(n_rows // tt,),
        in_specs=[pl.BlockSpec((1, ROPE_HALF), lambda i: (0, 0))],
        out_specs=[pl.BlockSpec((tt, ROPE_HALF), lambda i: (i, 0))] * 2,
        out_shape=[shape, shape],
        name="rope_tables",
    )(inv_freq.reshape(1, ROPE_HALF))


def _conv_taps(x, w_row, bias, n_rows):
    acc = None
    for r in range(SUBLANES):
        rows = n_rows if r == 0 else n_rows + SUBLANES
        g = None
        for p in range((CONV_WIDTH + CONV_LEAD + SUBLANES - 1) // SUBLANES):
            j = SUBLANES * p + r - CONV_LEAD
            if 0 <= j < CONV_WIDTH:
                term = w_row(j) * x[SUBLANES * p:SUBLANES * p + rows]
                g = term if g is None else g + term
        shifted = g[r:r + n_rows]
        acc = shifted if acc is None else acc + shifted
    return acc + bias


def _conv_kernel(cur_ref, prev_ref, hist_ref, w_ref, b_ref, o_ref, win_ref, *, tt, row_blk):
    t = pl.program_id(1)

    @pl.when(t == 0)
    def _():
        win_ref[CONV_LEAD:CONV_HALO_ROWS, :] = hist_ref[0]

    @pl.when(t > 0)
    def _():
        win_ref[0:CONV_HALO_ROWS, :] = prev_ref[0]

    win_ref[CONV_HALO_ROWS:CONV_HALO_ROWS + tt, :] = cur_ref[0]

    d = cur_ref.shape[-1]
    for cb in range(d // LANES):
        cs = slice(cb * LANES, (cb + 1) * LANES)
        for rb in range(tt // row_blk):
            x = win_ref[rb * row_blk:rb * row_blk + row_blk + CONV_HALO_ROWS, cs]
            o_ref[0, rb * row_blk:(rb + 1) * row_blk, cs] = _conv_taps(
                x, lambda j: w_ref[j:j + 1, cs], b_ref[:, cs], row_blk)


def _conv_module(a, hist, conv_dw, conv_dw_b, tt):
    bsz, seq, d = a.shape
    n_hist = CONV_WIDTH - 1
    halo_per_tile = tt // CONV_HALO_ROWS
    row_blk = min(tt, CONV_ROW_BLK)
    est = 2 * tt * d * 4 + 4 * CONV_HALO_ROWS * d * 4 + 2 * tt * d * 4 + (tt + CONV_HALO_ROWS) * d * 4
    const2 = lambda b, t: (0, 0)
    return pl.pallas_call(
        functools.partial(_conv_kernel, tt=tt, row_blk=row_blk),
        grid=(bsz, seq // tt),
        in_specs=[
            pl.BlockSpec((1, tt, d), lambda b, t: (b, t, 0)),
            pl.BlockSpec((1, CONV_HALO_ROWS, d),
                         lambda b, t: (b, jnp.maximum(t * halo_per_tile - 1, 0), 0)),
            pl.BlockSpec((1, n_hist, d), lambda b, t: (b, 0, 0)),
            pl.BlockSpec((CONV_WIDTH, d), const2),
            pl.BlockSpec((1, d), const2),
        ],
        out_specs=pl.BlockSpec((1, tt, d), lambda b, t: (b, t, 0)),
        out_shape=jax.ShapeDtypeStruct((bsz, seq, d), F32),
        scratch_shapes=[pltpu.VMEM((CONV_HALO_ROWS + tt, d), F32)],
        compiler_params=_params(est),
        name="conv_module",
    )(a, a, hist, conv_dw, conv_dw_b.reshape(1, d))


def _in_proj_conv_kernel(h_ref, w_ref, acur_ref, aprev_ref, hist_ref, cw_ref, cb_ref, cos_ref, sin_ref,
                         bg_ref, cpre_ref, big_ref, win_ref, *, tm, tiles_per_seq, bounds):
    i = pl.program_id(0)
    t = pl.program_id(1)
    k_start, v_start, sg_start, gate_start = bounds
    n_sub = tm // MM_SUB_ROWS
    conv_blks_per_sub = (tm // CONV_ROW_BLK) // n_sub

    @pl.when(i % tiles_per_seq == 0)
    def _():
        win_ref[CONV_LEAD:CONV_HALO_ROWS, :] = hist_ref[0, t]

    @pl.when(i % tiles_per_seq != 0)
    def _():
        win_ref[0:CONV_HALO_ROWS, :] = aprev_ref[...]

    win_ref[CONV_HALO_ROWS:CONV_HALO_ROWS + tm, :] = acur_ref[...]

    def step(epilogue):
        half = w_ref.shape[-1] // 2
        conv_blks_per_half = conv_blks_per_sub // 2

        def sub_tiles(it, carry):
            for u in range(IN_PROJ_UNROLL):
                r_sub = pl.multiple_of((it * IN_PROJ_UNROLL + u) * MM_SUB_ROWS, MM_SUB_ROWS)
                rs = pl.ds(r_sub, MM_SUB_ROWS)
                for c in range(2):
                    cs = slice(c * half, (c + 1) * half)
                    acc = _dot(h_ref[rs, :], w_ref[:, cs])
                    for b in range(conv_blks_per_half):
                        r0 = pl.multiple_of(r_sub + (c * conv_blks_per_half + b) * CONV_ROW_BLK, CONV_ROW_BLK)
                        x = win_ref[pl.ds(r0, CONV_ROW_BLK + CONV_HALO_ROWS), :]
                        cpre_ref[pl.ds(r0, CONV_ROW_BLK), :] = _conv_taps(
                            x, lambda j: cw_ref[t, j:j + 1, :], cb_ref[t], CONV_ROW_BLK)
                    big_ref[rs, cs] = epilogue(acc, rs, cs).astype(big_ref.dtype)
            return carry

        lax.fori_loop(0, n_sub // IN_PROJ_UNROLL, sub_tiles, 0)

    def epi_rotary(acc, rs, cs):
        scale = jnp.where(t >= k_start, RET_DK ** -0.5, 1.0).astype(F32)
        return _rotary(acc, cos_ref[rs, :], sin_ref[rs, :], scale)

    pl.when(t < v_start)(lambda: step(epi_rotary))
    pl.when((t >= v_start) & (t < sg_start))(lambda: step(lambda acc, rs, cs: acc))
    pl.when((t >= sg_start) & (t < gate_start))(lambda: step(lambda acc, rs, cs: _silu(acc)))
    pl.when(t >= gate_start)(lambda: step(lambda acc, rs, cs: _sigmoid(acc + bg_ref[t - gate_start, :, cs])))


def _in_proj_conv(h, w_in, a, hist, conv_dw, conv_dw_b, cos_t, sin_t, b_gate, seq, tm):
    rows, d = h.shape
    tn = IN_PROJ_TILE
    n_chunks = d // LANES
    d_qk = N_RET_HEADS * RET_DK
    d_v = N_RET_HEADS * RET_DV
    off_wide = 2 * d
    wide_cols = 2 * d_qk + 2 * d_v + 2 * d
    assert wide_cols // tn == n_chunks and seq % tm == 0 and tm % CONV_HALO_ROWS == 0
    tiles_per_seq = seq // tm
    tab_tiles = cos_t.shape[0] // tm
    halo_per_tile = tm // CONV_HALO_ROWS
    bounds = (d_qk // tn, 2 * d_qk // tn, (2 * d_qk + d_v) // tn, (2 * d_qk + 2 * d_v) // tn)
    n_bias_steps = 2 * d // tn
    whole3 = lambda i, t: (0, 0, 0)
    tab_map = lambda i, t: (i % tab_tiles, 0)
    cw = conv_dw.reshape(CONV_WIDTH, n_chunks, LANES).transpose(1, 0, 2)
    cb = conv_dw_b.reshape(n_chunks, 1, LANES)
    hist_cm = hist.reshape(hist.shape[0], CONV_WIDTH - 1, n_chunks, LANES).transpose(0, 2, 1, 3)
    est = (2 * tm * d * 2 + 2 * d * tn * 2 + 4 * tm * LANES * 4 + 4 * tm * ROPE_HALF * 4
           + 2 * tm * tn * 2 + (tm + CONV_HALO_ROWS) * LANES * 4 + 3 * MM_SUB_ROWS * tn * 4)
    return pl.pallas_call(
        functools.partial(_in_proj_conv_kernel, tm=tm, tiles_per_seq=tiles_per_seq, bounds=bounds),
        grid=(rows // tm, n_chunks),
        in_specs=[
            pl.BlockSpec((tm, d), lambda i, t: (i, 0)),
            pl.BlockSpec((d, tn), lambda i, t: (0, off_wide // tn + t)),
            pl.BlockSpec((tm, LANES), lambda i, t: (i, t)),
            pl.BlockSpec((CONV_HALO_ROWS, LANES), lambda i, t: (jnp.maximum(i * halo_per_tile - 1, 0), t)),
            pl.BlockSpec((1, n_chunks, CONV_WIDTH - 1, LANES), lambda i, t: (i // tiles_per_seq, 0, 0, 0)),
            pl.BlockSpec((n_chunks, CONV_WIDTH, LANES), whole3),
            pl.BlockSpec((n_chunks, 1, LANES), whole3),
            pl.BlockSpec((tm, ROPE_HALF), tab_map),
            pl.BlockSpec((tm, ROPE_HALF), tab_map),
            pl.BlockSpec((n_bias_steps, 1, tn), whole3),
        ],
        out_specs=[
            pl.BlockSpec((tm, LANES), lambda i, t: (i, t)),
            pl.BlockSpec((tm, tn), lambda i, t: (i, t)),
        ],
        out_shape=[jax.ShapeDtypeStruct((rows, d), F32),
                   jax.ShapeDtypeStruct((rows, wide_cols), BF16)],
        scratch_shapes=[pltpu.VMEM((CONV_HALO_ROWS + tm, LANES), F32)],
        compiler_params=_params(est),
        name="in_proj_conv",
    )(h, w_in, a, a, hist_cm, cw, cb, cos_t, sin_t, b_gate.reshape(n_bias_steps, 1, tn))


def _ln_mm_kernel(c_ref, g_ref, b_ref, w_ref, gate_ref, o_ref, c_scr, *, sub):
    j = pl.program_id(1)
    tm = c_ref.shape[0]

    @pl.when(j == 0)
    def _():
        for s in range(tm // sub):
            rs = slice(s * sub, (s + 1) * sub)
            c = c_ref[rs, :]
            mu = jnp.mean(c, axis=-1, keepdims=True)
            cc = c - mu
            var = jnp.mean(cc * cc, axis=-1, keepdims=True)
            y = cc * lax.rsqrt(var + EPS) * g_ref[...] + b_ref[...]
            act = _silu(y).astype(c_scr.dtype)
            c_scr[rs, :] = act
            o_ref[rs, :] = gate_ref[rs, :].astype(F32) * _dot(act, w_ref[...])

    @pl.when(j > 0)
    def _():
        o_ref[...] = gate_ref[...].astype(F32) * _dot(c_scr[...], w_ref[...])


def _conv_out_proj(c_pre, ln_g, ln_b, w, gate_src, tm, tn):
    rows, d = c_pre.shape
    n_out = w.shape[1]
    gates, gate_off = gate_src
    est = 2 * tm * d * 4 + 2 * d * tn * 2 + 2 * tm * tn * 2 + 2 * tm * tn * 4 + tm * d * 2 + 6 * NORM_SUB_ROWS * d * 4
    const2 = lambda i, j: (0, 0)
    return pl.pallas_call(
        functools.partial(_ln_mm_kernel, sub=min(tm, NORM_SUB_ROWS)),
        grid=(rows // tm, n_out // tn),
        in_specs=[pl.BlockSpec((tm, d), lambda i, j: (i, 0)),
                  pl.BlockSpec((1, d), const2),
                  pl.BlockSpec((1, d), const2),
                  pl.BlockSpec((d, tn), lambda i, j: (0, j)),
                  _tile_spec(tm, tn, gate_off)],
        out_specs=pl.BlockSpec((tm, tn), lambda i, j: (i, j)),
        out_shape=jax.ShapeDtypeStruct((rows, n_out), F32),
        scratch_shapes=[pltpu.VMEM((tm, d), BF16)],
        compiler_params=_params(est),
        name="conv_out_proj",
    )(c_pre, ln_g.reshape(1, d), ln_b.reshape(1, d), w, gates)


def _log_decay(h):
    return math.log1p(-(2.0 ** (-5.0 - h)))


def _retention_kernel(q_ref, k_ref, v_ref, sg_ref, s0_ref, gn_ref, o_ref, s_out_ref,
                      s_ref, decay_ref, *, chunk, n_chunks):
    b = pl.program_id(0)
    c = pl.program_id(1)

    @pl.when((b == 0) & (c == 0))
    def _():
        n = lax.broadcasted_iota(jnp.int32, (chunk, chunk), 0)
        m = lax.broadcasted_iota(jnp.int32, (chunk, chunk), 1)
        diff = (n - m).astype(F32)
        for h in range(N_RET_HEADS):
            decay_ref[h] = jnp.where(diff >= 0.0, jnp.exp(jnp.maximum(diff, 0.0) * _log_decay(h)), 0.0)

    @pl.when(c == 0)
    def _():
        s_ref[...] = s0_ref[0]

    idx = lax.broadcasted_iota(jnp.int32, (chunk, 1), 0).astype(F32)
    for h in range(N_RET_HEADS):
        lg = _log_decay(h)
        q = q_ref[:, h * RET_DK:(h + 1) * RET_DK]
        k = k_ref[:, h * RET_DK:(h + 1) * RET_DK]
        v = v_ref[:, h * RET_DV:(h + 1) * RET_DV]
        s_old = s_ref[h]
        scores = lax.dot_general(q, k, (((1,), (1,)), ((), ())), preferred_element_type=F32)
        scores = (scores * decay_ref[h]).astype(BF16)
        o = _dot(scores, v)
        xi = jnp.exp((idx + 1.0) * lg)
        o = o + _dot(q, s_old.astype(BF16)) * xi
        zeta = jnp.exp((chunk - 1.0 - idx) * lg)
        kz = (k.astype(F32) * zeta).astype(BF16)
        s_ref[h] = math.exp(chunk * lg) * s_old + lax.dot_general(
            kz, v, (((0,), (0,)), ((), ())), preferred_element_type=F32)
        mu = jnp.mean(o, axis=-1, keepdims=True)
        oc = o - mu
        var = jnp.mean(oc * oc, axis=-1, keepdims=True)
        on = oc * lax.rsqrt(var + EPS) * gn_ref[:, h * RET_DV:(h + 1) * RET_DV]
        gate = sg_ref[:, h * RET_DV:(h + 1) * RET_DV].astype(F32)
        o_ref[:, h * RET_DV:(h + 1) * RET_DV] = (on * gate).astype(o_ref.dtype)

    @pl.when(c == n_chunks - 1)
    def _():
        s_out_ref[0] = s_ref[...]


def _retention(q_src, k_src, v_src, sg_src, state, gn_g, bsz, seq, chunk):
    rows = bsz * seq
    n_chunks = seq // chunk
    dqk = N_RET_HEADS * RET_DK
    dv = N_RET_HEADS * RET_DV
    state_blk = (1, N_RET_HEADS, RET_DK, RET_DV)
    est = (2 * 2 * chunk * dqk * 2 + 3 * 2 * chunk * dv * 2 + 4 * _nbytes(state_blk, F32)
           + _nbytes(state_blk, F32) + N_RET_HEADS * chunk * chunk * 4 + 8 * chunk * RET_DV * 4)

    def src_spec(src, width):
        assert src[1] % width == 0
        return pl.BlockSpec((chunk, width), functools.partial(
            lambda b, c, o: (b * n_chunks + c, o), o=src[1] // width))

    return pl.pallas_call(
        functools.partial(_retention_kernel, chunk=chunk, n_chunks=n_chunks),
        grid=(bsz, n_chunks),
        in_specs=[
            src_spec(q_src, dqk), src_spec(k_src, dqk), src_spec(v_src, dv), src_spec(sg_src, dv),
            pl.BlockSpec(state_blk, lambda b, c: (b, 0, 0, 0)),
            pl.BlockSpec((1, dv), lambda b, c: (0, 0)),
        ],
        out_specs=[pl.BlockSpec((chunk, dv), lambda b, c: (b * n_chunks + c, 0)),
                   pl.BlockSpec(state_blk, lambda b, c: (b, 0, 0, 0))],
        out_shape=[jax.ShapeDtypeStruct((rows, dv), BF16),
                   jax.ShapeDtypeStruct((bsz,) + state_blk[1:], F32)],
        scratch_shapes=[pltpu.VMEM(state_blk[1:], F32),
                        pltpu.VMEM((N_RET_HEADS, chunk, chunk), F32)],
        compiler_params=_params(est),
        name="retention",
    )(q_src[0], k_src[0], v_src[0], sg_src[0], state, gn_g.reshape(1, dv))


def _mm_norm_kernel(*refs, n_k, emit_next, sub):
    if emit_next:
        a_ref, w_ref, res_ref, nw_ref, nw2_ref, o_ref, h_ref, acc_ref = refs
    else:
        a_ref, w_ref, res_ref, nw_ref, o_ref, acc_ref = refs
    kk = pl.program_id(1)
    tm = a_ref.shape[0]

    if n_k > 1:
        @pl.when(kk == 0)
        def _():
            acc_ref[...] = _dot(a_ref[...], w_ref[...])
    if n_k > 2:
        @pl.when((kk > 0) & (kk < n_k - 1))
        def _():
            acc_ref[...] += _dot(a_ref[...], w_ref[...])

    @pl.when(kk == n_k - 1)
    def _():
        for s in range(tm // sub):
            rs = slice(s * sub, (s + 1) * sub)
            m = _dot(a_ref[rs, :], w_ref[...])
            if n_k > 1:
                m = m + acc_ref[rs, :]
            inv = lax.rsqrt(jnp.mean(m * m, axis=-1, keepdims=True) + EPS)
            y = res_ref[rs, :] + m * inv * nw_ref[...]
            o_ref[rs, :] = y
            if emit_next:
                inv2 = lax.rsqrt(jnp.mean(y * y, axis=-1, keepdims=True) + EPS)
                h_ref[rs, :] = (y * inv2 * nw2_ref[...]).astype(h_ref.dtype)


def _matmul_norm_residual(a, w, res, norm_w, next_norm_w, tm, tk, name):
    rows, k = a.shape
    d = w.shape[1]
    n_k = k // tk
    emit_next = next_norm_w is not None
    row_i = lambda i, kk: (i, 0)
    const2 = lambda i, kk: (0, 0)
    w_buffers = 1 if n_k == 1 else 2
    in_specs = [pl.BlockSpec((tm, tk), lambda i, kk: (i, kk)),
                pl.BlockSpec((tk, d), lambda i, kk: (kk, 0), pipeline_mode=pl.Buffered(w_buffers)),
                pl.BlockSpec((tm, d), row_i),
                pl.BlockSpec((1, d), const2)]
    args = [a, w, res, norm_w.reshape(1, d)]
    out_specs = [pl.BlockSpec((tm, d), row_i)]
    out_shape = [jax.ShapeDtypeStruct((rows, d), F32)]
    acc_shape = (tm, d) if n_k > 1 else (SUBLANES, LANES)
    est = (2 * tm * tk * 2 + w_buffers * tk * d * 2 + 4 * tm * d * 4 + _nbytes(acc_shape, F32)
           + 4 * NORM_SUB_ROWS * d * 4)
    if emit_next:
        in_specs.append(pl.BlockSpec((1, d), const2))
        args.append(next_norm_w.reshape(1, d))
        out_specs.append(pl.BlockSpec((tm, d), row_i))
        out_shape.append(jax.ShapeDtypeStruct((rows, d), BF16))
        est += 2 * tm * d * 2
    outs = pl.pallas_call(
        functools.partial(_mm_norm_kernel, n_k=n_k, emit_next=emit_next, sub=min(tm, NORM_SUB_ROWS)),
        grid=(rows // tm, n_k),
        in_specs=in_specs,
        out_specs=out_specs,
        out_shape=out_shape,
        scratch_shapes=[pltpu.VMEM(acc_shape, F32)],
        compiler_params=_params(est),
        name=name,
    )(*args)
    return outs if emit_next else outs[0]


def _trunk_layer(x, conv_hist, ret_state, start, p):
    bsz, seq, d = x.shape
    rows = bsz * seq
    x2 = x.reshape(rows, d)
    d_qk = N_RET_HEADS * RET_DK
    d_v = N_RET_HEADS * RET_DV
    off_qk = 2 * d
    off_v = off_qk + 2 * d_qk
    off_sg = off_v + d_v
    off_gate = off_sg + d_v

    tm = min(rows, 1024)
    tm_big = 2 * tm if rows % (2 * tm) == 0 else tm
    tn = 1024
    w_in = p["w_in"]

    n_tab = max(seq, tm_big)
    cos_t, sin_t = _rope_tables(start, seq, n_tab, min(n_tab, 1024))

    a, h = _rms_glu(x2, p["norm_mix_pre"], w_in, 0, d, tm, 512)
    if seq % tm_big == 0:
        c_pre, big = _in_proj_conv(h, w_in, a, conv_hist, p["conv_dw"], p["conv_dw_b"],
                                   cos_t, sin_t, p["b_gate"], seq, tm_big)
        q_src, k_src = (big, 0), (big, d_qk)
        v_src, sg_src = (big, 2 * d_qk), (big, 2 * d_qk + d_v)
        gate_c_src, gate_r_src = (big, 2 * d_qk + 2 * d_v), (big, 2 * d_qk + 2 * d_v + d)
    else:
        c_pre = _conv_module(a.reshape(bsz, seq, d), conv_hist, p["conv_dw"], p["conv_dw_b"],
                             min(seq, 128)).reshape(rows, d)
        tab_tiles = n_tab // tm
        tab_spec = pl.BlockSpec((tm, ROPE_HALF), lambda i, j: (i % tab_tiles, 0))
        qk = _matmul(h, [(w_in, off_qk)], [(cos_t, tab_spec), (sin_t, tab_spec)],
                     functools.partial(_epi_rotary, n_q_tiles=d_qk // tn),
                     2 * d_qk, BF16, tm, tn, "in_proj_qk")
        v = _matmul(h, [(w_in, off_v)], [], _epi_identity, d_v, BF16, tm, tn, "in_proj_v")
        sg = _matmul(h, [(w_in, off_sg)], [], _epi_silu, d_v, BF16, tm, tn, "in_proj_swish_gate")
        gates = _matmul(h, [(w_in, off_gate)], [(p["b_gate"].reshape(1, 2 * d), _row_spec(tn, 0))],
                        _epi_sigmoid_bias, 2 * d, BF16, tm, tn, "in_proj_merge_gates")
        q_src, k_src, v_src, sg_src = (qk, 0), (qk, d_qk), (v, 0), (sg, 0)
        gate_c_src, gate_r_src = (gates, 0), (gates, d)

    new_hist = a.reshape(bsz, seq, d)[:, seq - (CONV_WIDTH - 1):, :]
    chunk = min(seq, 256)
    o, s_new = _retention(q_src, k_src, v_src, sg_src, ret_state, p["ret_gn_g"], bsz, seq, chunk)

    tm_n = min(rows, 512)
    y_c = _conv_out_proj(c_pre, p["conv_ln_g"], p["conv_ln_b"], p["w_conv_out"], gate_c_src, tm_n, d)
    mix_in = _matmul(o, [(p["w_ret_out"], 0)],
                     [(gate_r_src[0], _tile_spec(tm, tn, gate_r_src[1])), (y_c, _tile_spec(tm, tn, 0))],
                     _epi_gate_mul_add, d, BF16, tm, tn, "ret_out_proj")

    x1, h2 = _matmul_norm_residual(mix_in, p["w_out"], x2, p["norm_mix_post"], p["norm_ffn_pre"],
                                   tm_n, d, "out_proj_norm")

    d_ff = p["w_ffn_gate"].shape[1]
    f = _matmul(h2, [(p["w_ffn_gate"], 0), (p["w_ffn_up"], 0)], [], _epi_swiglu,
                d_ff, BF16, tm_big, 512, "ffn_up")
    y = _matmul_norm_residual(f, p["w_ffn_down"], x1, p["norm_ffn_post"], None, min(rows, 256), d_ff, "ffn_down_norm")
    return y.reshape(bsz, seq, d), new_hist, s_new


def kernel(x_prompt, x_sample, cache_conv, state_ret, norm_mix_pre, norm_mix_post, w_in, b_gate, conv_dw, conv_dw_b, conv_ln_g, conv_ln_b, w_conv_out, ret_gn_g, w_ret_out, w_out, norm_ffn_pre, norm_ffn_post, w_ffn_gate, w_ffn_up, w_ffn_down):
    depth = w_in.shape[0]
    past_len = 1024
    y_prompt, y_sample = x_prompt, x_sample
    conv_p, ret_p, conv_s, ret_s = [], [], [], []
    n_batch = x_prompt.shape[0]
    for l in range(depth):
        p = {
            "norm_mix_pre": norm_mix_pre[l], "norm_mix_post": norm_mix_post[l],
            "w_in": w_in[l].astype(BF16), "b_gate": b_gate[l],
            "conv_dw": conv_dw[l], "conv_dw_b": conv_dw_b[l],
            "conv_ln_g": conv_ln_g[l], "conv_ln_b": conv_ln_b[l],
            "w_conv_out": w_conv_out[l].astype(BF16), "ret_gn_g": ret_gn_g[l],
            "w_ret_out": w_ret_out[l].astype(BF16), "w_out": w_out[l].astype(BF16),
            "norm_ffn_pre": norm_ffn_pre[l], "norm_ffn_post": norm_ffn_post[l],
            "w_ffn_gate": w_ffn_gate[l].astype(BF16), "w_ffn_up": w_ffn_up[l].astype(BF16),
            "w_ffn_down": w_ffn_down[l].astype(BF16),
        }
        zero_hist = jnp.zeros((n_batch, CONV_WIDTH - 1, x_prompt.shape[-1]), x_prompt.dtype)
        zero_state = jnp.zeros((n_batch, N_RET_HEADS, RET_DK, RET_DV), x_prompt.dtype)
        y_prompt, hp, sp = _trunk_layer(y_prompt, zero_hist, zero_state, 0, p)
        y_sample, hs, ss = _trunk_layer(y_sample, cache_conv[l], state_ret[l], past_len, p)
        conv_p.append(hp)
        ret_p.append(sp)
        conv_s.append(hs)
        ret_s.append(ss)
    return (y_prompt, y_sample, jnp.stack(conv_p), jnp.stack(ret_p), jnp.stack(conv_s), jnp.stack(ret_s))
```

```python
import functools
import math

import jax
import jax.numpy as jnp
from jax import lax
from jax.experimental import pallas as pl
from jax.experimental.pallas import tpu as pltpu

F32 = jnp.float32
BF16 = jnp.bfloat16

EPS = 1e-6
ROPE_BASE = 10000.0
CONV_WIDTH = 31
N_RET_HEADS = 8
RET_DK = 256
RET_DV = 512
ROPE_HALF = RET_DK // 2

MIB = 1024 * 1024
V7X_VMEM_REQUEST_CAP = 56 * MIB
SUBLANES = 8
LANES = 128
CONV_HALO_ROWS = 32
CONV_LEAD = CONV_HALO_ROWS - (CONV_WIDTH - 1)
CONV_ROW_BLK = 64
NORM_SUB_ROWS = 128
MM_SUB_ROWS = 256
IN_PROJ_UNROLL = 4
IN_PROJ_TILE = 1024


def _params(est_bytes):
    limit = min(V7X_VMEM_REQUEST_CAP, max(32 * MIB, int(est_bytes * 1.25)))
    return pltpu.CompilerParams(vmem_limit_bytes=limit)


def _nbytes(shape, dtype):
    return math.prod(shape) * jnp.dtype(dtype).itemsize


def _sigmoid(x):
    return jax.nn.sigmoid(x)


def _silu(x):
    return x * jax.nn.sigmoid(x)


def _dot(a, b):
    return jnp.dot(a, b, preferred_element_type=F32)


def _rms_glu_kernel(x_ref, nw_ref, wv_ref, wg_ref, a_ref, h_ref, *, sub):
    j = pl.program_id(1)
    tm = x_ref.shape[0]

    def glu(rs, h):
        a_ref[rs, :] = _dot(h, wv_ref[...]) * _sigmoid(_dot(h, wg_ref[...]))

    @pl.when(j == 0)
    def _():
        for s in range(tm // sub):
            rs = slice(s * sub, (s + 1) * sub)
            x = x_ref[rs, :]
            ms = jnp.mean(x * x, axis=-1, keepdims=True)
            h = (x * lax.rsqrt(ms + EPS) * nw_ref[...]).astype(h_ref.dtype)
            h_ref[rs, :] = h
            glu(rs, h)

    @pl.when(j > 0)
    def _():
        for s in range(tm // sub):
            rs = slice(s * sub, (s + 1) * sub)
            glu(rs, h_ref[rs, :])


def _rms_glu(x, norm_w, w_in, off_value, off_gate, tm, tn):
    rows, d = x.shape
    est = 2 * tm * d * 4 + 4 * d * tn * 2 + 2 * tm * tn * 4 + 2 * tm * d * 2 + 3 * MM_SUB_ROWS * tn * 4
    w_spec = lambda off: pl.BlockSpec((d, tn), functools.partial(lambda i, j, o: (0, o + j), o=off // tn))
    return pl.pallas_call(
        functools.partial(_rms_glu_kernel, sub=min(tm, MM_SUB_ROWS)),
        grid=(rows // tm, d // tn),
        in_specs=[pl.BlockSpec((tm, d), lambda i, j: (i, 0)),
                  pl.BlockSpec((1, d), lambda i, j: (0, 0)),
                  w_spec(off_value), w_spec(off_gate)],
        out_specs=[pl.BlockSpec((tm, tn), lambda i, j: (i, j)),
                   pl.BlockSpec((tm, d), lambda i, j: (i, 0))],
        out_shape=[jax.ShapeDtypeStruct((rows, d), F32),
                   jax.ShapeDtypeStruct((rows, d), BF16)],
        compiler_params=_params(est),
        name="rms_glu",
    )(x, norm_w.reshape(1, d), w_in, w_in)


def _mm_kernel(*refs, n_w, epilogue, sub):
    a_ref = refs[0]
    w_refs = refs[1:1 + n_w]
    extra_refs = refs[1 + n_w:-1]
    o_ref = refs[-1]
    for s in range(a_ref.shape[0] // sub):
        rs = slice(s * sub, (s + 1) * sub)
        a = a_ref[rs, :]
        accs = [_dot(a, w_ref[...]) for w_ref in w_refs]
        o_ref[rs, :] = epilogue(accs, extra_refs, rs).astype(o_ref.dtype)


def _matmul(a, ws, extras, epilogue, n_out, out_dtype, tm, tn, name):
    rows, k = a.shape
    grid = (rows // tm, n_out // tn)
    in_specs = [pl.BlockSpec((tm, k), lambda i, j: (i, 0))]
    args = [a]
    est = 2 * tm * k * 2
    for w, off in ws:
        assert off % tn == 0
        in_specs.append(pl.BlockSpec((k, tn), functools.partial(lambda i, j, o: (0, o + j), o=off // tn)))
        args.append(w)
        est += 2 * k * tn * 2
    for arr, spec in extras:
        in_specs.append(spec)
        args.append(arr)
        est += 2 * _nbytes(spec.block_shape, arr.dtype)
    est += 2 * tm * tn * jnp.dtype(out_dtype).itemsize
    est += (len(ws) + 1) * tm * tn * 4
    return pl.pallas_call(
        functools.partial(_mm_kernel, n_w=len(ws), epilogue=epilogue, sub=min(tm, MM_SUB_ROWS)),
        grid=grid,
        in_specs=in_specs,
        out_specs=pl.BlockSpec((tm, tn), lambda i, j: (i, j)),
        out_shape=jax.ShapeDtypeStruct((rows, n_out), out_dtype),
        compiler_params=_params(est),
        name=name,
    )(*args)


def _tile_spec(tm, tn, off):
    assert off % tn == 0
    return pl.BlockSpec((tm, tn), functools.partial(lambda i, j, o: (i, o + j), o=off // tn))


def _row_spec(tn, off):
    return pl.BlockSpec((1, tn), functools.partial(lambda i, j, o: (0, o + j), o=off // tn))


def _rotary(acc, cos, sin, scale):
    outs = []
    for h in range(acc.shape[-1] // RET_DK):
        x1 = acc[:, h * RET_DK:h * RET_DK + ROPE_HALF]
        x2 = acc[:, h * RET_DK + ROPE_HALF:(h + 1) * RET_DK]
        outs.append((x1 * cos - x2 * sin) * scale)
        outs.append((x1 * sin + x2 * cos) * scale)
    return jnp.concatenate(outs, axis=-1)


def _epi_identity(accs, extras, rs):
    return accs[0]


def _epi_silu(accs, extras, rs):
    return _silu(accs[0])


def _epi_sigmoid_bias(accs, extras, rs):
    return _sigmoid(accs[0] + extras[0][...])


def _epi_rotary(accs, extras, rs, *, n_q_tiles):
    scale = jnp.where(pl.program_id(1) >= n_q_tiles, RET_DK ** -0.5, 1.0).astype(F32)
    return _rotary(accs[0], extras[0][rs, :], extras[1][rs, :], scale)


def _epi_gate_mul_add(accs, extras, rs):
    return extras[0][rs, :].astype(F32) * accs[0] + extras[1][rs, :]


def _epi_swiglu(accs, extras, rs):
    return _silu(accs[0]) * accs[1]


def _rope_kernel(inv_ref, cos_ref, sin_ref, *, start, seq_len, tt):
    row = pl.program_id(0) * tt + lax.broadcasted_iota(jnp.int32, (tt, ROPE_HALF), 0)
    pos = (start + lax.rem(row, seq_len)).astype(F32)
    ang = pos * inv_ref[...]
    cos_ref[...] = jnp.cos(ang)
    sin_ref[...] = jnp.sin(ang)


def _rope_tables(start, seq_len, n_rows, tt):
    inv_freq = 1.0 / (ROPE_BASE ** (jnp.arange(ROPE_HALF, dtype=F32) / ROPE_HALF))
    shape = jax.ShapeDtypeStruct((n_rows, ROPE_HALF), F32)
    return pl.pallas_call(
        functools.partial(_rope_kernel, start=start, seq_len=seq_len, tt=tt),
        grid=(n_rows // tt,),
        in_specs=[pl.BlockSpec((1, ROPE_HALF), lambda i: (0, 0))],
        out_specs=[pl.BlockSpec((tt, ROPE_HALF), lambda i: (i, 0))] * 2,
        out_shape=[shape, shape],
        name="rope_tables",
    )(inv_freq.reshape(1, ROPE_HALF))


def _conv_taps(x, w_row, bias, n_rows):
    acc = None
    for r in range(SUBLANES):
        rows = n_rows if r == 0 else n_rows + SUBLANES
        g = None
        for p in range((CONV_WIDTH + CONV_LEAD + SUBLANES - 1) // SUBLANES):
            j = SUBLANES * p + r - CONV_LEAD
            if 0 <= j < CONV_WIDTH:
                term = w_row(j) * x[SUBLANES * p:SUBLANES * p + rows]
                g = term if g is None else g + term
        shifted = g[r:r + n_rows]
        acc = shifted if acc is None else acc + shifted
    return acc + bias


def _conv_kernel(cur_ref, prev_ref, hist_ref, w_ref, b_ref, o_ref, win_ref, *, tt, row_blk):
    t = pl.program_id(1)

    @pl.when(t == 0)
    def _():
        win_ref[CONV_LEAD:CONV_HALO_ROWS, :] = hist_ref[0]

    @pl.when(t > 0)
    def _():
        win_ref[0:CONV_HALO_ROWS, :] = prev_ref[0]

    win_ref[CONV_HALO_ROWS:CONV_HALO_ROWS + tt, :] = cur_ref[0]

    d = cur_ref.shape[-1]
    for cb in range(d // LANES):
        cs = slice(cb * LANES, (cb + 1) * LANES)
        for rb in range(tt // row_blk):
            x = win_ref[rb * row_blk:rb * row_blk + row_blk + CONV_HALO_ROWS, cs]
            o_ref[0, rb * row_blk:(rb + 1) * row_blk, cs] = _conv_taps(
                x, lambda j: w_ref[j:j + 1, cs], b_ref[:, cs], row_blk)


def _conv_module(a, hist, conv_dw, conv_dw_b, tt):
    bsz, seq, d = a.shape
    n_hist = CONV_WIDTH - 1
    halo_per_tile = tt // CONV_HALO_ROWS
    row_blk = min(tt, CONV_ROW_BLK)
    est = 2 * tt * d * 4 + 4 * CONV_HALO_ROWS * d * 4 + 2 * tt * d * 4 + (tt + CONV_HALO_ROWS) * d * 4
    const2 = lambda b, t: (0, 0)
    return pl.pallas_call(
        functools.partial(_conv_kernel, tt=tt, row_blk=row_blk),
        grid=(bsz, seq // tt),
        in_specs=[
            pl.BlockSpec((1, tt, d), lambda b, t: (b, t, 0)),
            pl.BlockSpec((1, CONV_HALO_ROWS, d),
                         lambda b, t: (b, jnp.maximum(t * halo_per_tile - 1, 0), 0)),
            pl.BlockSpec((1, n_hist, d), lambda b, t: (b, 0, 0)),
            pl.BlockSpec((CONV_WIDTH, d), const2),
            pl.BlockSpec((1, d), const2),
        ],
        out_specs=pl.BlockSpec((1, tt, d), lambda b, t: (b, t, 0)),
        out_shape=jax.ShapeDtypeStruct((bsz, seq, d), F32),
        scratch_shapes=[pltpu.VMEM((CONV_HALO_ROWS + tt, d), F32)],
        compiler_params=_params(est),
        name="conv_module",
    )(a, a, hist, conv_dw, conv_dw_b.reshape(1, d))


def _in_proj_conv_kernel(h_ref, w_ref, acur_ref, aprev_ref, hist_ref, cw_ref, cb_ref, cos_ref, sin_ref,
                         bg_ref, cpre_ref, big_ref, win_ref, *, tm, tiles_per_seq, bounds):
    i = pl.program_id(0)
    t = pl.program_id(1)
    k_start, v_start, sg_start, gate_start = bounds
    n_sub = tm // MM_SUB_ROWS
    conv_blks_per_sub = (tm // CONV_ROW_BLK) // n_sub

    @pl.when(i % tiles_per_seq == 0)
    def _():
        win_ref[CONV_LEAD:CONV_HALO_ROWS, :] = hist_ref[0, t]

    @pl.when(i % tiles_per_seq != 0)
    def _():
        win_ref[0:CONV_HALO_ROWS, :] = aprev_ref[...]

    win_ref[CONV_HALO_ROWS:CONV_HALO_ROWS + tm, :] = acur_ref[...]

    def step(epilogue):
        half = w_ref.shape[-1] // 2
        conv_blks_per_half = conv_blks_per_sub // 2

        def sub_tiles(it, carry):
            for u in range(IN_PROJ_UNROLL):
                r_sub = pl.multiple_of((it * IN_PROJ_UNROLL + u) * MM_SUB_ROWS, MM_SUB_ROWS)
                rs = pl.ds(r_sub, MM_SUB_ROWS)
                for c in range(2):
                    cs = slice(c * half, (c + 1) * half)
                    acc = _dot(h_ref[rs, :], w_ref[:, cs])
                    for b in range(conv_blks_per_half):
                        r0 = pl.multiple_of(r_sub + (c * conv_blks_per_half + b) * CONV_ROW_BLK, CONV_ROW_BLK)
                        x = win_ref[pl.ds(r0, CONV_ROW_BLK + CONV_HALO_ROWS), :]
                        cpre_ref[pl.ds(r0, CONV_ROW_BLK), :] = _conv_taps(
                            x, lambda j: cw_ref[t, j:j + 1, :], cb_ref[t], CONV_ROW_BLK)
                    big_ref[rs, cs] = epilogue(acc, rs, cs).astype(big_ref.dtype)
            return carry

        lax.fori_loop(0, n_sub // IN_PROJ_UNROLL, sub_tiles, 0)

    def epi_rotary(acc, rs, cs):
        scale = jnp.where(t >= k_start, RET_DK ** -0.5, 1.0).astype(F32)
        return _rotary(acc, cos_ref[rs, :], sin_ref[rs, :], scale)

    pl.when(t < v_start)(lambda: step(epi_rotary))
    pl.when((t >= v_start) & (t < sg_start))(lambda: step(lambda acc, rs, cs: acc))
    pl.when((t >= sg_start) & (t < gate_start))(lambda: step(lambda acc, rs, cs: _silu(acc)))
    pl.when(t >= gate_start)(lambda: step(lambda acc, rs, cs: _sigmoid(acc + bg_ref[t - gate_start, :, cs])))


def _in_proj_conv(h, w_in, a, hist, conv_dw, conv_dw_b, cos_t, sin_t, b_gate, seq, tm):
    rows, d = h.shape
    tn = IN_PROJ_TILE
    n_chunks = d // LANES
    d_qk = N_RET_HEADS * RET_DK
    d_v = N_RET_HEADS * RET_DV
    off_wide = 2 * d
    wide_cols = 2 * d_qk + 2 * d_v + 2 * d
    assert wide_cols // tn == n_chunks and seq % tm == 0 and tm % CONV_HALO_ROWS == 0
    tiles_per_seq = seq // tm
    tab_tiles = cos_t.shape[0] // tm
    halo_per_tile = tm // CONV_HALO_ROWS
    bounds = (d_qk // tn, 2 * d_qk // tn, (2 * d_qk + d_v) // tn, (2 * d_qk + 2 * d_v) // tn)
    n_bias_steps = 2 * d // tn
    whole3 = lambda i, t: (0, 0, 0)
    tab_map = lambda i, t: (i % tab_tiles, 0)
    cw = conv_dw.reshape(CONV_WIDTH, n_chunks, LANES).transpose(1, 0, 2)
    cb = conv_dw_b.reshape(n_chunks, 1, LANES)
    hist_cm = hist.reshape(hist.shape[0], CONV_WIDTH - 1, n_chunks, LANES).transpose(0, 2, 1, 3)
    est = (2 * tm * d * 2 + 2 * d * tn * 2 + 4 * tm * LANES * 4 + 4 * tm * ROPE_HALF * 4
           + 2 * tm * tn * 2 + (tm + CONV_HALO_ROWS) * LANES * 4 + 3 * MM_SUB_ROWS * tn * 4)
    return pl.pallas_call(
        functools.partial(_in_proj_conv_kernel, tm=tm, tiles_per_seq=tiles_per_seq, bounds=bounds),
        grid=(rows // tm, n_chunks),
        in_specs=[
            pl.BlockSpec((tm, d), lambda i, t: (i, 0)),
            pl.BlockSpec((d, tn), lambda i, t: (0, off_wide // tn + t)),
            pl.BlockSpec((tm, LANES), lambda i, t: (i, t)),
            pl.BlockSpec((CONV_HALO_ROWS, LANES), lambda i, t: (jnp.maximum(i * halo_per_tile - 1, 0), t)),
            pl.BlockSpec((1, n_chunks, CONV_WIDTH - 1, LANES), lambda i, t: (i // tiles_per_seq, 0, 0, 0)),
            pl.BlockSpec((n_chunks, CONV_WIDTH, LANES), whole3),
            pl.BlockSpec((n_chunks, 1, LANES), whole3),
            pl.BlockSpec((tm, ROPE_HALF), tab_map),
            pl.BlockSpec((tm, ROPE_HALF), tab_map),
            pl.BlockSpec((n_bias_steps, 1, tn), whole3),
        ],
        out_specs=[
            pl.BlockSpec((tm, LANES), lambda i, t: (i, t)),
            pl.BlockSpec((tm, tn), lambda i, t: (i, t)),
        ],
        out_shape=[jax.ShapeDtypeStruct((rows, d), F32),
                   jax.ShapeDtypeStruct((rows, wide_cols), BF16)],
        scratch_shapes=[pltpu.VMEM((CONV_HALO_ROWS + tm, LANES), F32)],
        compiler_params=_params(est),
        name="in_proj_conv",
    )(h, w_in, a, a, hist_cm, cw, cb, cos_t, sin_t, b_gate.reshape(n_bias_steps, 1, tn))


def _ln_mm_kernel(c_ref, g_ref, b_ref, w_ref, gate_ref, o_ref, c_scr, *, sub):
    j = pl.program_id(1)
    tm = c_ref.shape[0]

    @pl.when(j == 0)
    def _():
        for s in range(tm // sub):
            rs = slice(s * sub, (s + 1) * sub)
            c = c_ref[rs, :]
            mu = jnp.mean(c, axis=-1, keepdims=True)
            cc = c - mu
            var = jnp.mean(cc * cc, axis=-1, keepdims=True)
            y = cc * lax.rsqrt(var + EPS) * g_ref[...] + b_ref[...]
            act = _silu(y).astype(c_scr.dtype)
            c_scr[rs, :] = act
            o_ref[rs, :] = gate_ref[rs, :].astype(F32) * _dot(act, w_ref[...])

    @pl.when(j > 0)
    def _():
        o_ref[...] = gate_ref[...].astype(F32) * _dot(c_scr[...], w_ref[...])


def _conv_out_proj(c_pre, ln_g, ln_b, w, gate_src, tm, tn):
    rows, d = c_pre.shape
    n_out = w.shape[1]
    gates, gate_off = gate_src
    est = 2 * tm * d * 4 + 2 * d * tn * 2 + 2 * tm * tn * 2 + 2 * tm * tn * 4 + tm * d * 2 + 6 * NORM_SUB_ROWS * d * 4
    const2 = lambda i, j: (0, 0)
    return pl.pallas_call(
        functools.partial(_ln_mm_kernel, sub=min(tm, NORM_SUB_ROWS)),
        grid=(rows // tm, n_out // tn),
        in_specs=[pl.BlockSpec((tm, d), lambda i, j: (i, 0)),
                  pl.BlockSpec((1, d), const2),
                  pl.BlockSpec((1, d), const2),
                  pl.BlockSpec((d, tn), lambda i, j: (0, j)),
                  _tile_spec(tm, tn, gate_off)],
        out_specs=pl.BlockSpec((tm, tn), lambda i, j: (i, j)),
        out_shape=jax.ShapeDtypeStruct((rows, n_out), F32),
        scratch_shapes=[pltpu.VMEM((tm, d), BF16)],
        compiler_params=_params(est),
        name="conv_out_proj",
    )(c_pre, ln_g.reshape(1, d), ln_b.reshape(1, d), w, gates)


def _log_decay(h):
    return math.log1p(-(2.0 ** (-5.0 - h)))


def _retention_kernel(q_ref, k_ref, v_ref, sg_ref, s0_ref, gn_ref, o_ref, s_out_ref,
                      s_ref, decay_ref, *, chunk, n_chunks):
    b = pl.program_id(0)
    c = pl.program_id(1)

    @pl.when((b == 0) & (c == 0))
    def _():
        n = lax.broadcasted_iota(jnp.int32, (chunk, chunk), 0)
        m = lax.broadcasted_iota(jnp.int32, (chunk, chunk), 1)
        diff = (n - m).astype(F32)
        for h in range(N_RET_HEADS):
            decay_ref[h] = jnp.where(diff >= 0.0, jnp.exp(jnp.maximum(diff, 0.0) * _log_decay(h)), 0.0)

    @pl.when(c == 0)
    def _():
        s_ref[...] = s0_ref[0]

    idx = lax.broadcasted_iota(jnp.int32, (chunk, 1), 0).astype(F32)
    for h in range(N_RET_HEADS):
        lg = _log_decay(h)
        q = q_ref[:, h * RET_DK:(h + 1) * RET_DK]
        k = k_ref[:, h * RET_DK:(h + 1) * RET_DK]
        v = v_ref[:, h * RET_DV:(h + 1) * RET_DV]
        s_old = s_ref[h]
        scores = lax.dot_general(q, k, (((1,), (1,)), ((), ())), preferred_element_type=F32)
        scores = (scores * decay_ref[h]).astype(BF16)
        o = _dot(scores, v)
        xi = jnp.exp((idx + 1.0) * lg)
        o = o + _dot(q, s_old.astype(BF16)) * xi
        zeta = jnp.exp((chunk - 1.0 - idx) * lg)
        kz = (k.astype(F32) * zeta).astype(BF16)
        s_ref[h] = math.exp(chunk * lg) * s_old + lax.dot_general(
            kz, v, (((0,), (0,)), ((), ())), preferred_element_type=F32)
        mu = jnp.mean(o, axis=-1, keepdims=True)
        oc = o - mu
        var = jnp.mean(oc * oc, axis=-1, keepdims=True)
        on = oc * lax.rsqrt(var + EPS) * gn_ref[:, h * RET_DV:(h + 1) * RET_DV]
        gate = sg_ref[:, h * RET_DV:(h + 1) * RET_DV].astype(F32)
        o_ref[:, h * RET_DV:(h + 1) * RET_DV] = (on * gate).astype(o_ref.dtype)

    @pl.when(c == n_chunks - 1)
    def _():
        s_out_ref[0] = s_ref[...]


def _retention(q_src, k_src, v_src, sg_src, state, gn_g, bsz, seq, chunk):
    rows = bsz * seq
    n_chunks = seq // chunk
    dqk = N_RET_HEADS * RET_DK
    dv = N_RET_HEADS * RET_DV
    state_blk = (1, N_RET_HEADS, RET_DK, RET_DV)
    est = (2 * 2 * chunk * dqk * 2 + 3 * 2 * chunk * dv * 2 + 4 * _nbytes(state_blk, F32)
           + _nbytes(state_blk, F32) + N_RET_HEADS * chunk * chunk * 4 + 8 * chunk * RET_DV * 4)

    def src_spec(src, width):
        assert src[1] % width == 0
        return pl.BlockSpec((chunk, width), functools.partial(
            lambda b, c, o: (b * n_chunks + c, o), o=src[1] // width))

    return pl.pallas_call(
        functools.partial(_retention_kernel, chunk=chunk, n_chunks=n_chunks),
        grid=(bsz, n_chunks),
        in_specs=[
            src_spec(q_src, dqk), src_spec(k_src, dqk), src_spec(v_src, dv), src_spec(sg_src, dv),
            pl.BlockSpec(state_blk, lambda b, c: (b, 0, 0, 0)),
            pl.BlockSpec((1, dv), lambda b, c: (0, 0)),
        ],
        out_specs=[pl.BlockSpec((chunk, dv), lambda b, c: (b * n_chunks + c, 0)),
                   pl.BlockSpec(state_blk, lambda b, c: (b, 0, 0, 0))],
        out_shape=[jax.ShapeDtypeStruct((rows, dv), BF16),
                   jax.ShapeDtypeStruct((bsz,) + state_blk[1:], F32)],
        scratch_shapes=[pltpu.VMEM(state_blk[1:], F32),
                        pltpu.VMEM((N_RET_HEADS, chunk, chunk), F32)],
        compiler_params=_params(est),
        name="retention",
    )(q_src[0], k_src[0], v_src[0], sg_src[0], state, gn_g.reshape(1, dv))


def _mm_norm_kernel(*refs, n_k, emit_next, sub):
    if emit_next:
        a_ref, w_ref, res_ref, nw_ref, nw2_ref, o_ref, h_ref, acc_ref = refs
    else:
        a_ref, w_ref, res_ref, nw_ref, o_ref, acc_ref = refs
    kk = pl.program_id(1)
    tm = a_ref.shape[0]

    if n_k > 1:
        @pl.when(kk == 0)
        def _():
            acc_ref[...] = _dot(a_ref[...], w_ref[...])
    if n_k > 2:
        @pl.when((kk > 0) & (kk < n_k - 1))
        def _():
            acc_ref[...] += _dot(a_ref[...], w_ref[...])

    @pl.when(kk == n_k - 1)
    def _():
        for s in range(tm // sub):
            rs = slice(s * sub, (s + 1) * sub)
            m = _dot(a_ref[rs, :], w_ref[...])
            if n_k > 1:
                m = m + acc_ref[rs, :]
            inv = lax.rsqrt(jnp.mean(m * m, axis=-1, keepdims=True) + EPS)
            y = res_ref[rs, :] + m * inv * nw_ref[...]
            o_ref[rs, :] = y
            if emit_next:
                inv2 = lax.rsqrt(jnp.mean(y * y, axis=-1, keepdims=True) + EPS)
                h_ref[rs, :] = (y * inv2 * nw2_ref[...]).astype(h_ref.dtype)


def _matmul_norm_residual(a, w, res, norm_w, next_norm_w, tm, tk, name):
    rows, k = a.shape
    d = w.shape[1]
    n_k = k // tk
    emit_next = next_norm_w is not None
    row_i = lambda i, kk: (i, 0)
    const2 = lambda i, kk: (0, 0)
    w_buffers = 1 if n_k == 1 else 2
    in_specs = [pl.BlockSpec((tm, tk), lambda i, kk: (i, kk)),
                pl.BlockSpec((tk, d), lambda i, kk: (kk, 0), pipeline_mode=pl.Buffered(w_buffers)),
                pl.BlockSpec((tm, d), row_i),
                pl.BlockSpec((1, d), const2)]
    args = [a, w, res, norm_w.reshape(1, d)]
    out_specs = [pl.BlockSpec((tm, d), row_i)]
    out_shape = [jax.ShapeDtypeStruct((rows, d), F32)]
    acc_shape = (tm, d) if n_k > 1 else (SUBLANES, LANES)
    est = (2 * tm * tk * 2 + w_buffers * tk * d * 2 + 4 * tm * d * 4 + _nbytes(acc_shape, F32)
           + 4 * NORM_SUB_ROWS * d * 4)
    if emit_next:
        in_specs.append(pl.BlockSpec((1, d), const2))
        args.append(next_norm_w.reshape(1, d))
        out_specs.append(pl.BlockSpec((tm, d), row_i))
        out_shape.append(jax.ShapeDtypeStruct((rows, d), BF16))
        est += 2 * tm * d * 2
    outs = pl.pallas_call(
        functools.partial(_mm_norm_kernel, n_k=n_k, emit_next=emit_next, sub=min(tm, NORM_SUB_ROWS)),
        grid=(rows // tm, n_k),
        in_specs=in_specs,
        out_specs=out_specs,
        out_shape=out_shape,
        scratch_shapes=[pltpu.VMEM(acc_shape, F32)],
        compiler_params=_params(est),
        name=name,
    )(*args)
    return outs if emit_next else outs[0]


def _trunk_layer(x, conv_hist, ret_state, start, p):
    bsz, seq, d = x.shape
    rows = bsz * seq
    x2 = x.reshape(rows, d)
    d_qk = N_RET_HEADS * RET_DK
    d_v = N_RET_HEADS * RET_DV
    off_qk = 2 * d
    off_v = off_qk + 2 * d_qk
    off_sg = off_v + d_v
    off_gate = off_sg + d_v

    tm = min(rows, 1024)
    tm_big = 2 * tm if rows % (2 * tm) == 0 else tm
    tn = 1024
    w_in = p["w_in"]

    n_tab = max(seq, tm_big)
    cos_t, sin_t = _rope_tables(start, seq, n_tab, min(n_tab, 1024))

    a, h = _rms_glu(x2, p["norm_mix_pre"], w_in, 0, d, tm, tn)
    if seq % tm_big == 0:
        c_pre, big = _in_proj_conv(h, w_in, a, conv_hist, p["conv_dw"], p["conv_dw_b"],
                                   cos_t, sin_t, p["b_gate"], seq, tm_big)
        q_src, k_src = (big, 0), (big, d_qk)
        v_src, sg_src = (big, 2 * d_qk), (big, 2 * d_qk + d_v)
        gate_c_src, gate_r_src = (big, 2 * d_qk + 2 * d_v), (big, 2 * d_qk + 2 * d_v + d)
    else:
        c_pre = _conv_module(a.reshape(bsz, seq, d), conv_hist, p["conv_dw"], p["conv_dw_b"],
                             min(seq, 128)).reshape(rows, d)
        tab_tiles = n_tab // tm
        tab_spec = pl.BlockSpec((tm, ROPE_HALF), lambda i, j: (i % tab_tiles, 0))
        qk = _matmul(h, [(w_in, off_qk)], [(cos_t, tab_spec), (sin_t, tab_spec)],
                     functools.partial(_epi_rotary, n_q_tiles=d_qk // tn),
                     2 * d_qk, BF16, tm, tn, "in_proj_qk")
        v = _matmul(h, [(w_in, off_v)], [], _epi_identity, d_v, BF16, tm, tn, "in_proj_v")
        sg = _matmul(h, [(w_in, off_sg)], [], _epi_silu, d_v, BF16, tm, tn, "in_proj_swish_gate")
        gates = _matmul(h, [(w_in, off_gate)], [(p["b_gate"].reshape(1, 2 * d), _row_spec(tn, 0))],
                        _epi_sigmoid_bias, 2 * d, BF16, tm, tn, "in_proj_merge_gates")
        q_src, k_src, v_src, sg_src = (qk, 0), (qk, d_qk), (v, 0), (sg, 0)
        gate_c_src, gate_r_src = (gates, 0), (gates, d)

    new_hist = a.reshape(bsz, seq, d)[:, seq - (CONV_WIDTH - 1):, :]
    chunk = min(seq, 256)
    o, s_new = _retention(q_src, k_src, v_src, sg_src, ret_state, p["ret_gn_g"], bsz, seq, chunk)

    tm_n = min(rows, 512)
    y_c = _conv_out_proj(c_pre, p["conv_ln_g"], p["conv_ln_b"], p["w_conv_out"], gate_c_src, tm_n, d)
    mix_in = _matmul(o, [(p["w_ret_out"], 0)],
                     [(gate_r_src[0], _tile_spec(tm, tn, gate_r_src[1])), (y_c, _tile_spec(tm, tn, 0))],
                     _epi_gate_mul_add, d, BF16, tm, tn, "ret_out_proj")

    x1, h2 = _matmul_norm_residual(mix_in, p["w_out"], x2, p["norm_mix_post"], p["norm_ffn_pre"],
                                   tm_n, d, "out_proj_norm")

    d_ff = p["w_ffn_gate"].shape[1]
    f = _matmul(h2, [(p["w_ffn_gate"], 0), (p["w_ffn_up"], 0)], [], _epi_swiglu,
                d_ff, BF16, tm_big, 512, "ffn_up")
    y = _matmul_norm_residual(f, p["w_ffn_down"], x1, p["norm_ffn_post"], None, min(rows, 256), d_ff, "ffn_down_norm")
    return y.reshape(bsz, seq, d), new_hist, s_new


def kernel(x_prompt, x_sample, cache_conv, state_ret, norm_mix_pre, norm_mix_post, w_in, b_gate, conv_dw, conv_dw_b, conv_ln_g, conv_ln_b, w_conv_out, ret_gn_g, w_ret_out, w_out, norm_ffn_pre, norm_ffn_post, w_ffn_gate, w_ffn_up, w_ffn_down):
    depth = w_in.shape[0]
    past_len = 1024
    y_prompt, y_sample = x_prompt, x_sample
    conv_p, ret_p, conv_s, ret_s = [], [], [], []
    n_batch = x_prompt.shape[0]
    for l in range(depth):
        p = {
            "norm_mix_pre": norm_mix_pre[l], "norm_mix_post": norm_mix_post[l],
            "w_in": w_in[l].astype(BF16), "b_gate": b_gate[l],
            "conv_dw": conv_dw[l], "conv_dw_b": conv_dw_b[l],
            "conv_ln_g": conv_ln_g[l], "conv_ln_b": conv_ln_b[l],
            "w_conv_out": w_conv_out[l].astype(BF16), "ret_gn_g": ret_gn_g[l],
            "w_ret_out": w_ret_out[l].astype(BF16), "w_out": w_out[l].astype(BF16),
            "norm_ffn_pre": norm_ffn_pre[l], "norm_ffn_post": norm_ffn_post[l],
            "w_ffn_gate": w_ffn_gate[l].astype(BF16), "w_ffn_up": w_ffn_up[l].astype(BF16),
            "w_ffn_down": w_ffn_down[l].astype(BF16),
        }
        zero_hist = jnp.zeros((n_batch, CONV_WIDTH - 1, x_prompt.shape[-1]), x_prompt.dtype)
        zero_state = jnp.zeros((n_batch, N_RET_HEADS, RET_DK, RET_DV), x_prompt.dtype)
        y_prompt, hp, sp = _trunk_layer(y_prompt, zero_hist, zero_state, 0, p)
        y_sample, hs, ss = _trunk_layer(y_sample, cache_conv[l], state_ret[l], past_len, p)
        conv_p.append(hp)
        ret_p.append(sp)
        conv_s.append(hs)
        ret_s.append(ss)
    return (y_prompt, y_sample, jnp.stack(conv_p), jnp.stack(ret_p), jnp.stack(conv_s), jnp.stack(ret_s))
```

```python
import functools
import math

import jax
import jax.numpy as jnp
from jax import lax
from jax.experimental import pallas as pl
from jax.experimental.pallas import tpu as pltpu

F32 = jnp.float32
BF16 = jnp.bfloat16

EPS = 1e-6
ROPE_BASE = 10000.0
CONV_WIDTH = 31
N_RET_HEADS = 8
RET_DK = 256
RET_DV = 512
ROPE_HALF = RET_DK // 2

MIB = 1024 * 1024
V7X_VMEM_REQUEST_CAP = 56 * MIB
SUBLANES = 8
LANES = 128
CONV_HALO_ROWS = 32
CONV_LEAD = CONV_HALO_ROWS - (CONV_WIDTH - 1)
CONV_ROW_BLK = 64
NORM_SUB_ROWS = 128
MM_SUB_ROWS = 256
IN_PROJ_UNROLL = 4
IN_PROJ_TILE = 1024


def _params(est_bytes):
    limit = min(V7X_VMEM_REQUEST_CAP, max(32 * MIB, int(est_bytes * 1.25)))
    return pltpu.CompilerParams(vmem_limit_bytes=limit)


def _nbytes(shape, dtype):
    return math.prod(shape) * jnp.dtype(dtype).itemsize


def _sigmoid(x):
    return jax.nn.sigmoid(x)


def _silu(x):
    return x * jax.nn.sigmoid(x)


def _dot(a, b):
    return jnp.dot(a, b, preferred_element_type=F32)


def _rms_glu_kernel(x_ref, nw_ref, wv_ref, wg_ref, a_ref, h_ref, *, sub):
    j = pl.program_id(1)
    tm = x_ref.shape[0]

    def glu(rs, h):
        a_ref[rs, :] = _dot(h, wv_ref[...]) * _sigmoid(_dot(h, wg_ref[...]))

    @pl.when(j == 0)
    def _():
        for s in range(tm // sub):
            rs = slice(s * sub, (s + 1) * sub)
            x = x_ref[rs, :]
            ms = jnp.mean(x * x, axis=-1, keepdims=True)
            h = (x * lax.rsqrt(ms + EPS) * nw_ref[...]).astype(h_ref.dtype)
            h_ref[rs, :] = h
            glu(rs, h)

    @pl.when(j > 0)
    def _():
        for s in range(tm // sub):
            rs = slice(s * sub, (s + 1) * sub)
            glu(rs, h_ref[rs, :])


def _rms_glu(x, norm_w, w_in, off_value, off_gate, tm, tn):
    rows, d = x.shape
    est = 2 * tm * d * 4 + 4 * d * tn * 2 + 2 * tm * tn * 4 + 2 * tm * d * 2 + 3 * MM_SUB_ROWS * tn * 4
    w_spec = lambda off: pl.BlockSpec((d, tn), functools.partial(lambda i, j, o: (0, o + j), o=off // tn))
    return pl.pallas_call(
        functools.partial(_rms_glu_kernel, sub=min(tm, MM_SUB_ROWS)),
        grid=(rows // tm, d // tn),
        in_specs=[pl.BlockSpec((tm, d), lambda i, j: (i, 0)),
                  pl.BlockSpec((1, d), lambda i, j: (0, 0)),
                  w_spec(off_value), w_spec(off_gate)],
        out_specs=[pl.BlockSpec((tm, tn), lambda i, j: (i, j)),
                   pl.BlockSpec((tm, d), lambda i, j: (i, 0))],
        out_shape=[jax.ShapeDtypeStruct((rows, d), F32),
                   jax.ShapeDtypeStruct((rows, d), BF16)],
        compiler_params=_params(est),
        name="rms_glu",
    )(x, norm_w.reshape(1, d), w_in, w_in)


def _mm_kernel(*refs, n_w, epilogue, sub):
    a_ref = refs[0]
    w_refs = refs[1:1 + n_w]
    extra_refs = refs[1 + n_w:-1]
    o_ref = refs[-1]
    for s in range(a_ref.shape[0] // sub):
        rs = slice(s * sub, (s + 1) * sub)
        a = a_ref[rs, :]
        accs = [_dot(a, w_ref[...]) for w_ref in w_refs]
        o_ref[rs, :] = epilogue(accs, extra_refs, rs).astype(o_ref.dtype)


def _matmul(a, ws, extras, epilogue, n_out, out_dtype, tm, tn, name):
    rows, k = a.shape
    grid = (rows // tm, n_out // tn)
    in_specs = [pl.BlockSpec((tm, k), lambda i, j: (i, 0))]
    args = [a]
    est = 2 * tm * k * 2
    for w, off in ws:
        assert off % tn == 0
        in_specs.append(pl.BlockSpec((k, tn), functools.partial(lambda i, j, o: (0, o + j), o=off // tn)))
        args.append(w)
        est += 2 * k * tn * 2
    for arr, spec in extras:
        in_specs.append(spec)
        args.append(arr)
        est += 2 * _nbytes(spec.block_shape, arr.dtype)
    est += 2 * tm * tn * jnp.dtype(out_dtype).itemsize
    est += (len(ws) + 1) * tm * tn * 4
    return pl.pallas_call(
        functools.partial(_mm_kernel, n_w=len(ws), epilogue=epilogue, sub=min(tm, MM_SUB_ROWS)),
        grid=grid,
        in_specs=in_specs,
        out_specs=pl.BlockSpec((tm, tn), lambda i, j: (i, j)),
        out_shape=jax.ShapeDtypeStruct((rows, n_out), out_dtype),
        compiler_params=_params(est),
        name=name,
    )(*args)


def _tile_spec(tm, tn, off):
    assert off % tn == 0
    return pl.BlockSpec((tm, tn), functools.partial(lambda i, j, o: (i, o + j), o=off // tn))


def _row_spec(tn, off):
    return pl.BlockSpec((1, tn), functools.partial(lambda i, j, o: (0, o + j), o=off // tn))


def _rotary(acc, cos, sin, scale):
    outs = []
    for h in range(acc.shape[-1] // RET_DK):
        x1 = acc[:, h * RET_DK:h * RET_DK + ROPE_HALF]
        x2 = acc[:, h * RET_DK + ROPE_HALF:(h + 1) * RET_DK]
        outs.append((x1 * cos - x2 * sin) * scale)
        outs.append((x1 * sin + x2 * cos) * scale)
    return jnp.concatenate(outs, axis=-1)


def _epi_identity(accs, extras, rs):
    return accs[0]


def _epi_silu(accs, extras, rs):
    return _silu(accs[0])


def _epi_sigmoid_bias(accs, extras, rs):
    return _sigmoid(accs[0] + extras[0][...])


def _epi_rotary(accs, extras, rs, *, n_q_tiles):
    scale = jnp.where(pl.program_id(1) >= n_q_tiles, RET_DK ** -0.5, 1.0).astype(F32)
    return _rotary(accs[0], extras[0][rs, :], extras[1][rs, :], scale)


def _epi_gate_mul_add(accs, extras, rs):
    return extras[0][rs, :].astype(F32) * accs[0] + extras[1][rs, :]


def _epi_swiglu(accs, extras, rs):
    return _silu(accs[0]) * accs[1]


def _rope_kernel(inv_ref, cos_ref, sin_ref, *, start, seq_len, tt):
    row = pl.program_id(0) * tt + lax.broadcasted_iota(jnp.int32, (tt, ROPE_HALF), 0)
    pos = (start + lax.rem(row, seq_len)).astype(F32)
    ang = pos * inv_ref[...]
    cos_ref[...] = jnp.cos(ang)
    sin_ref[...] = jnp.sin(ang)


def _rope_tables(start, seq_len, n_rows, tt):
    inv_freq = 1.0 / (ROPE_BASE ** (jnp.arange(ROPE_HALF, dtype=F32) / ROPE_HALF))
    shape = jax.ShapeDtypeStruct((n_rows, ROPE_HALF), F32)
    return pl.pallas_call(
        functools.partial(_rope_kernel, start=start, seq_len=seq_len, tt=tt),
        grid=(n_rows // tt,),
        in_specs=[pl.BlockSpec((1, ROPE_HALF), lambda i: (0, 0))],
        out_specs=[pl.BlockSpec((tt, ROPE_HALF), lambda i: (i, 0))] * 2,
        out_shape=[shape, shape],
        name="rope_tables",
    )(inv_freq.reshape(1, ROPE_HALF))


def _conv_taps(x, w_row, bias, n_rows):
    acc = None
    for r in range(SUBLANES):
        rows = n_rows if r == 0 else n_rows + SUBLANES
        g = None
        for p in range((CONV_WIDTH + CONV_LEAD + SUBLANES - 1) // SUBLANES):
            j = SUBLANES * p + r - CONV_LEAD
            if 0 <= j < CONV_WIDTH:
                term = w_row(j) * x[SUBLANES * p:SUBLANES * p + rows]
                g = term if g is None else g + term
        shifted = g[r:r + n_rows]
        acc = shifted if acc is None else acc + shifted
    return acc + bias


def _conv_kernel(cur_ref, prev_ref, hist_ref, w_ref, b_ref, o_ref, win_ref, *, tt, row_blk):
    t = pl.program_id(1)

    @pl.when(t == 0)
    def _():
        win_ref[CONV_LEAD:CONV_HALO_ROWS, :] = hist_ref[0]

    @pl.when(t > 0)
    def _():
        win_ref[0:CONV_HALO_ROWS, :] = prev_ref[0]

    win_ref[CONV_HALO_ROWS:CONV_HALO_ROWS + tt, :] = cur_ref[0]

    d = cur_ref.shape[-1]
    for cb in range(d // LANES):
        cs = slice(cb * LANES, (cb + 1) * LANES)
        for rb in range(tt // row_blk):
            x = win_ref[rb * row_blk:rb * row_blk + row_blk + CONV_HALO_ROWS, cs]
            o_ref[0, rb * row_blk:(rb + 1) * row_blk, cs] = _conv_taps(
                x, lambda j: w_ref[j:j + 1, cs], b_ref[:, cs], row_blk)


def _conv_module(a, hist, conv_dw, conv_dw_b, tt):
    bsz, seq, d = a.shape
    n_hist = CONV_WIDTH - 1
    halo_per_tile = tt // CONV_HALO_ROWS
    row_blk = min(tt, CONV_ROW_BLK)
    est = 2 * tt * d * 4 + 4 * CONV_HALO_ROWS * d * 4 + 2 * tt * d * 4 + (tt + CONV_HALO_ROWS) * d * 4
    const2 = lambda b, t: (0, 0)
    return pl.pallas_call(
        functools.partial(_conv_kernel, tt=tt, row_blk=row_blk),
        grid=(bsz, seq // tt),
        in_specs=[
            pl.BlockSpec((1, tt, d), lambda b, t: (b, t, 0)),
            pl.BlockSpec((1, CONV_HALO_ROWS, d),
                         lambda b, t: (b, jnp.maximum(t * halo_per_tile - 1, 0), 0)),
            pl.BlockSpec((1, n_hist, d), lambda b, t: (b, 0, 0)),
            pl.BlockSpec((CONV_WIDTH, d), const2),
            pl.BlockSpec((1, d), const2),
        ],
        out_specs=pl.BlockSpec((1, tt, d), lambda b, t: (b, t, 0)),
        out_shape=jax.ShapeDtypeStruct((bsz, seq, d), F32),
        scratch_shapes=[pltpu.VMEM((CONV_HALO_ROWS + tt, d), F32)],
        compiler_params=_params(est),
        name="conv_module",
    )(a, a, hist, conv_dw, conv_dw_b.reshape(1, d))


def _in_proj_conv_kernel(h_ref, w_ref, acur_ref, aprev_ref, hist_ref, cw_ref, cb_ref, cos_ref, sin_ref,
                         bg_ref, cpre_ref, big_ref, win_ref, *, tm, tiles_per_seq, bounds):
    i = pl.program_id(0)
    t = pl.program_id(1)
    k_start, v_start, sg_start, gate_start = bounds
    n_sub = tm // MM_SUB_ROWS
    conv_blks_per_sub = (tm // CONV_ROW_BLK) // n_sub

    @pl.when(i % tiles_per_seq == 0)
    def _():
        win_ref[CONV_LEAD:CONV_HALO_ROWS, :] = hist_ref[0, t]

    @pl.when(i % tiles_per_seq != 0)
    def _():
        win_ref[0:CONV_HALO_ROWS, :] = aprev_ref[...]

    win_ref[CONV_HALO_ROWS:CONV_HALO_ROWS + tm, :] = acur_ref[...]

    def step(epilogue):
        half = w_ref.shape[-1] // 2
        conv_blks_per_half = conv_blks_per_sub // 2

        def sub_tiles(it, carry):
            for u in range(IN_PROJ_UNROLL):
                r_sub = pl.multiple_of((it * IN_PROJ_UNROLL + u) * MM_SUB_ROWS, MM_SUB_ROWS)
                rs = pl.ds(r_sub, MM_SUB_ROWS)
                for c in range(2):
                    cs = slice(c * half, (c + 1) * half)
                    acc = _dot(h_ref[rs, :], w_ref[:, cs])
                    for b in range(conv_blks_per_half):
                        r0 = pl.multiple_of(r_sub + (c * conv_blks_per_half + b) * CONV_ROW_BLK, CONV_ROW_BLK)
                        x = win_ref[pl.ds(r0, CONV_ROW_BLK + CONV_HALO_ROWS), :]
                        cpre_ref[pl.ds(r0, CONV_ROW_BLK), :] = _conv_taps(
                            x, lambda j: cw_ref[t, j:j + 1, :], cb_ref[t], CONV_ROW_BLK)
                    big_ref[rs, cs] = epilogue(acc, rs, cs).astype(big_ref.dtype)
            return carry

        lax.fori_loop(0, n_sub // IN_PROJ_UNROLL, sub_tiles, 0)

    def epi_rotary(acc, rs, cs):
        scale = jnp.where(t >= k_start, RET_DK ** -0.5, 1.0).astype(F32)
        return _rotary(acc, cos_ref[rs, :], sin_ref[rs, :], scale)

    pl.when(t < v_start)(lambda: step(epi_rotary))
    pl.when((t >= v_start) & (t < sg_start))(lambda: step(lambda acc, rs, cs: acc))
    pl.when((t >= sg_start) & (t < gate_start))(lambda: step(lambda acc, rs, cs: _silu(acc)))
    pl.when(t >= gate_start)(lambda: step(lambda acc, rs, cs: _sigmoid(acc + bg_ref[t - gate_start, :, cs])))


def _in_proj_conv(h, w_in, a, hist, conv_dw, conv_dw_b, cos_t, sin_t, b_gate, seq, tm):
    rows, d = h.shape
    tn = IN_PROJ_TILE
    n_chunks = d // LANES
    d_qk = N_RET_HEADS * RET_DK
    d_v = N_RET_HEADS * RET_DV
    off_wide = 2 * d
    wide_cols = 2 * d_qk + 2 * d_v + 2 * d
    assert wide_cols // tn == n_chunks and seq % tm == 0 and tm % CONV_HALO_ROWS == 0
    tiles_per_seq = seq // tm
    tab_tiles = cos_t.shape[0] // tm
    halo_per_tile = tm // CONV_HALO_ROWS
    bounds = (d_qk // tn, 2 * d_qk // tn, (2 * d_qk + d_v) // tn, (2 * d_qk + 2 * d_v) // tn)
    n_bias_steps = 2 * d // tn
    whole3 = lambda i, t: (0, 0, 0)
    tab_map = lambda i, t: (i % tab_tiles, 0)
    cw = conv_dw.reshape(CONV_WIDTH, n_chunks, LANES).transpose(1, 0, 2)
    cb = conv_dw_b.reshape(n_chunks, 1, LANES)
    hist_cm = hist.reshape(hist.shape[0], CONV_WIDTH - 1, n_chunks, LANES).transpose(0, 2, 1, 3)
    est = (2 * tm * d * 2 + 2 * d * tn * 2 + 4 * tm * LANES * 4 + 4 * tm * ROPE_HALF * 4
           + 2 * tm * tn * 2 + (tm + CONV_HALO_ROWS) * LANES * 4 + 3 * MM_SUB_ROWS * tn * 4)
    return pl.pallas_call(
        functools.partial(_in_proj_conv_kernel, tm=tm, tiles_per_seq=tiles_per_seq, bounds=bounds),
        grid=(rows // tm, n_chunks),
        in_specs=[
            pl.BlockSpec((tm, d), lambda i, t: (i, 0)),
            pl.BlockSpec((d, tn), lambda i, t: (0, off_wide // tn + t)),
            pl.BlockSpec((tm, LANES), lambda i, t: (i, t)),
            pl.BlockSpec((CONV_HALO_ROWS, LANES), lambda i, t: (jnp.maximum(i * halo_per_tile - 1, 0), t)),
            pl.BlockSpec((1, n_chunks, CONV_WIDTH - 1, LANES), lambda i, t: (i // tiles_per_seq, 0, 0, 0)),
            pl.BlockSpec((n_chunks, CONV_WIDTH, LANES), whole3),
            pl.BlockSpec((n_chunks, 1, LANES), whole3),
            pl.BlockSpec((tm, ROPE_HALF), tab_map),
            pl.BlockSpec((tm, ROPE_HALF), tab_map),
            pl.BlockSpec((n_bias_steps, 1, tn), whole3),
        ],
        out_specs=[
            pl.BlockSpec((tm, LANES), lambda i, t: (i, t)),
            pl.BlockSpec((tm, tn), lambda i, t: (i, t)),
        ],
        out_shape=[jax.ShapeDtypeStruct((rows, d), F32),
                   jax.ShapeDtypeStruct((rows, wide_cols), BF16)],
        scratch_shapes=[pltpu.VMEM((CONV_HALO_ROWS + tm, LANES), F32)],
        compiler_params=_params(est),
        name="in_proj_conv",
    )(h, w_in, a, a, hist_cm, cw, cb, cos_t, sin_t, b_gate.reshape(n_bias_steps, 1, tn))


def _ln_mm_kernel(c_ref, g_ref, b_ref, w_ref, gate_ref, o_ref, c_scr, *, sub):
    j = pl.program_id(1)
    tm = c_ref.shape[0]

    @pl.when(j == 0)
    def _():
        for s in range(tm // sub):
            rs = slice(s * sub, (s + 1) * sub)
            c = c_ref[rs, :]
            mu = jnp.mean(c, axis=-1, keepdims=True)
            cc = c - mu
            var = jnp.mean(cc * cc, axis=-1, keepdims=True)
            y = cc * lax.rsqrt(var + EPS) * g_ref[...] + b_ref[...]
            act = _silu(y).astype(c_scr.dtype)
            c_scr[rs, :] = act
            o_ref[rs, :] = gate_ref[rs, :].astype(F32) * _dot(act, w_ref[...])

    @pl.when(j > 0)
    def _():
        o_ref[...] = gate_ref[...].astype(F32) * _dot(c_scr[...], w_ref[...])


def _conv_out_proj(c_pre, ln_g, ln_b, w, gate_src, tm, tn):
    rows, d = c_pre.shape
    n_out = w.shape[1]
    gates, gate_off = gate_src
    est = 2 * tm * d * 4 + 2 * d * tn * 2 + 2 * tm * tn * 2 + 2 * tm * tn * 4 + tm * d * 2 + 6 * NORM_SUB_ROWS * d * 4
    const2 = lambda i, j: (0, 0)
    return pl.pallas_call(
        functools.partial(_ln_mm_kernel, sub=min(tm, NORM_SUB_ROWS)),
        grid=(rows // tm, n_out // tn),
        in_specs=[pl.BlockSpec((tm, d), lambda i, j: (i, 0)),
                  pl.BlockSpec((1, d), const2),
                  pl.BlockSpec((1, d), const2),
                  pl.BlockSpec((d, tn), lambda i, j: (0, j)),
                  _tile_spec(tm, tn, gate_off)],
        out_specs=pl.BlockSpec((tm, tn), lambda i, j: (i, j)),
        out_shape=jax.ShapeDtypeStruct((rows, n_out), F32),
        scratch_shapes=[pltpu.VMEM((tm, d), BF16)],
        compiler_params=_params(est),
        name="conv_out_proj",
    )(c_pre, ln_g.reshape(1, d), ln_b.reshape(1, d), w, gates)


def _log_decay(h):
    return math.log1p(-(2.0 ** (-5.0 - h)))


def _retention_kernel(q_ref, k_ref, v_ref, sg_ref, s0_ref, gn_ref, o_ref, s_out_ref,
                      s_ref, decay_ref, *, chunk, n_chunks):
    b = pl.program_id(0)
    c = pl.program_id(1)

    @pl.when((b == 0) & (c == 0))
    def _():
        n = lax.broadcasted_iota(jnp.int32, (chunk, chunk), 0)
        m = lax.broadcasted_iota(jnp.int32, (chunk, chunk), 1)
        diff = (n - m).astype(F32)
        for h in range(N_RET_HEADS):
            decay_ref[h] = jnp.where(diff >= 0.0, jnp.exp(jnp.maximum(diff, 0.0) * _log_decay(h)), 0.0)

    @pl.when(c == 0)
    def _():
        s_ref[...] = s0_ref[0]

    idx = lax.broadcasted_iota(jnp.int32, (chunk, 1), 0).astype(F32)
    for h in range(N_RET_HEADS):
        lg = _log_decay(h)
        q = q_ref[:, h * RET_DK:(h + 1) * RET_DK]
        k = k_ref[:, h * RET_DK:(h + 1) * RET_DK]
        v = v_ref[:, h * RET_DV:(h + 1) * RET_DV]
        s_old = s_ref[h]
        scores = lax.dot_general(q, k, (((1,), (1,)), ((), ())), preferred_element_type=F32)
        scores = (scores * decay_ref[h]).astype(BF16)
        o = _dot(scores, v)
        xi = jnp.exp((idx + 1.0) * lg)
        o = o + _dot(q, s_old.astype(BF16)) * xi
        zeta = jnp.exp((chunk - 1.0 - idx) * lg)
        kz = (k.astype(F32) * zeta).astype(BF16)
        s_ref[h] = math.exp(chunk * lg) * s_old + lax.dot_general(
            kz, v, (((0,), (0,)), ((), ())), preferred_element_type=F32)
        mu = jnp.mean(o, axis=-1, keepdims=True)
        oc = o - mu
        var = jnp.mean(oc * oc, axis=-1, keepdims=True)
        on = oc * lax.rsqrt(var + EPS) * gn_ref[:, h * RET_DV:(h + 1) * RET_DV]
        gate = sg_ref[:, h * RET_DV:(h + 1) * RET_DV].astype(F32)
        o_ref[:, h * RET_DV:(h + 1) * RET_DV] = (on * gate).astype(o_ref.dtype)

    @pl.when(c == n_chunks - 1)
    def _():
        s_out_ref[0] = s_ref[...]


def _retention(q_src, k_src, v_src, sg_src, state, gn_g, bsz, seq, chunk):
    rows = bsz * seq
    n_chunks = seq // chunk
    dqk = N_RET_HEADS * RET_DK
    dv = N_RET_HEADS * RET_DV
    state_blk = (1, N_RET_HEADS, RET_DK, RET_DV)
    est = (2 * 2 * chunk * dqk * 2 + 3 * 2 * chunk * dv * 2 + 4 * _nbytes(state_blk, F32)
           + _nbytes(state_blk, F32) + N_RET_HEADS * chunk * chunk * 4 + 8 * chunk * RET_DV * 4)

    def src_spec(src, width):
        assert src[1] % width == 0
        return pl.BlockSpec((chunk, width), functools.partial(
            lambda b, c, o: (b * n_chunks + c, o), o=src[1] // width))

    return pl.pallas_call(
        functools.partial(_retention_kernel, chunk=chunk, n_chunks=n_chunks),
        grid=(bsz, n_chunks),
        in_specs=[
            src_spec(q_src, dqk), src_spec(k_src, dqk), src_spec(v_src, dv), src_spec(sg_src, dv),
            pl.BlockSpec(state_blk, lambda b, c: (b, 0, 0, 0)),
            pl.BlockSpec((1, dv), lambda b, c: (0, 0)),
        ],
        out_specs=[pl.BlockSpec((chunk, dv), lambda b, c: (b * n_chunks + c, 0)),
                   pl.BlockSpec(state_blk, lambda b, c: (b, 0, 0, 0))],
        out_shape=[jax.ShapeDtypeStruct((rows, dv), BF16),
                   jax.ShapeDtypeStruct((bsz,) + state_blk[1:], F32)],
        scratch_shapes=[pltpu.VMEM(state_blk[1:], F32),
                        pltpu.VMEM((N_RET_HEADS, chunk, chunk), F32)],
        compiler_params=_params(est),
        name="retention",
    )(q_src[0], k_src[0], v_src[0], sg_src[0], state, gn_g.reshape(1, dv))


def _mm_norm_kernel(*refs, n_k, emit_next, sub):
    if emit_next:
        a_ref, w_ref, res_ref, nw_ref, nw2_ref, o_ref, h_ref, acc_ref = refs
    else:
        a_ref, w_ref, res_ref, nw_ref, o_ref, acc_ref = refs
    kk = pl.program_id(1)
    tm = a_ref.shape[0]

    if n_k > 1:
        @pl.when(kk == 0)
        def _():
            acc_ref[...] = _dot(a_ref[...], w_ref[...])
    if n_k > 2:
        @pl.when((kk > 0) & (kk < n_k - 1))
        def _():
            acc_ref[...] += _dot(a_ref[...], w_ref[...])

    @pl.when(kk == n_k - 1)
    def _():
        for s in range(tm // sub):
            rs = slice(s * sub, (s + 1) * sub)
            m = _dot(a_ref[rs, :], w_ref[...])
            if n_k > 1:
                m = m + acc_ref[rs, :]
            inv = lax.rsqrt(jnp.mean(m * m, axis=-1, keepdims=True) + EPS)
            y = res_ref[rs, :] + m * inv * nw_ref[...]
            o_ref[rs, :] = y
            if emit_next:
                inv2 = lax.rsqrt(jnp.mean(y * y, axis=-1, keepdims=True) + EPS)
                h_ref[rs, :] = (y * inv2 * nw2_ref[...]).astype(h_ref.dtype)


def _matmul_norm_residual(a, w, res, norm_w, next_norm_w, tm, tk, name):
    rows, k = a.shape
    d = w.shape[1]
    n_k = k // tk
    emit_next = next_norm_w is not None
    row_i = lambda i, kk: (i, 0)
    const2 = lambda i, kk: (0, 0)
    w_buffers = 1 if n_k == 1 else 2
    in_specs = [pl.BlockSpec((tm, tk), lambda i, kk: (i, kk)),
                pl.BlockSpec((tk, d), lambda i, kk: (kk, 0), pipeline_mode=pl.Buffered(w_buffers)),
                pl.BlockSpec((tm, d), row_i),
                pl.BlockSpec((1, d), const2)]
    args = [a, w, res, norm_w.reshape(1, d)]
    out_specs = [pl.BlockSpec((tm, d), row_i)]
    out_shape = [jax.ShapeDtypeStruct((rows, d), F32)]
    acc_shape = (tm, d) if n_k > 1 else (SUBLANES, LANES)
    est = (2 * tm * tk * 2 + w_buffers * tk * d * 2 + 4 * tm * d * 4 + _nbytes(acc_shape, F32)
           + 4 * NORM_SUB_ROWS * d * 4)
    if emit_next:
        in_specs.append(pl.BlockSpec((1, d), const2))
        args.append(next_norm_w.reshape(1, d))
        out_specs.append(pl.BlockSpec((tm, d), row_i))
        out_shape.append(jax.ShapeDtypeStruct((rows, d), BF16))
        est += 2 * tm * d * 2
    outs = pl.pallas_call(
        functools.partial(_mm_norm_kernel, n_k=n_k, emit_next=emit_next, sub=min(tm, NORM_SUB_ROWS)),
        grid=(rows // tm, n_k),
        in_specs=in_specs,
        out_specs=out_specs,
        out_shape=out_shape,
        scratch_shapes=[pltpu.VMEM(acc_shape, F32)],
        compiler_params=_params(est),
        name=name,
    )(*args)
    return outs if emit_next else outs[0]


def _trunk_layer(x, conv_hist, ret_state, start, p):
    bsz, seq, d = x.shape
    rows = bsz * seq
    x2 = x.reshape(rows, d)
    d_qk = N_RET_HEADS * RET_DK
    d_v = N_RET_HEADS * RET_DV
    off_qk = 2 * d
    off_v = off_qk + 2 * d_qk
    off_sg = off_v + d_v
    off_gate = off_sg + d_v

    tm = min(rows, 1024)
    tm_big = 2 * tm if rows % (2 * tm) == 0 else tm
    tn = 1024
    w_in = p["w_in"]

    n_tab = max(seq, tm_big)
    cos_t, sin_t = _rope_tables(start, seq, n_tab, min(n_tab, 1024))

    a, h = _rms_glu(x2, p["norm_mix_pre"], w_in, 0, d, tm, 512)
    c_pre = _conv_module(a.reshape(bsz, seq, d), conv_hist, p["conv_dw"], p["conv_dw_b"],
                         min(seq, 256)).reshape(rows, d)
    tab_tiles = n_tab // tm_big
    tab_spec = pl.BlockSpec((tm_big, ROPE_HALF), lambda i, j: (i % tab_tiles, 0))
    qk = _matmul(h, [(w_in, off_qk)], [(cos_t, tab_spec), (sin_t, tab_spec)],
                 functools.partial(_epi_rotary, n_q_tiles=d_qk // tn),
                 2 * d_qk, BF16, tm_big, tn, "in_proj_qk")
    v = _matmul(h, [(w_in, off_v)], [], _epi_identity, d_v, BF16, tm_big, tn, "in_proj_v")
    sg = _matmul(h, [(w_in, off_sg)], [], _epi_silu, d_v, BF16, tm_big, tn, "in_proj_swish_gate")
    gates = _matmul(h, [(w_in, off_gate)], [(p["b_gate"].reshape(1, 2 * d), _row_spec(tn, 0))],
                    _epi_sigmoid_bias, 2 * d, BF16, tm_big, tn, "in_proj_merge_gates")
    q_src, k_src, v_src, sg_src = (qk, 0), (qk, d_qk), (v, 0), (sg, 0)
    gate_c_src, gate_r_src = (gates, 0), (gates, d)

    new_hist = a.reshape(bsz, seq, d)[:, seq - (CONV_WIDTH - 1):, :]
    chunk = min(seq, 256)
    o, s_new = _retention(q_src, k_src, v_src, sg_src, ret_state, p["ret_gn_g"], bsz, seq, chunk)

    tm_n = min(rows, 512)
    y_c = _conv_out_proj(c_pre, p["conv_ln_g"], p["conv_ln_b"], p["w_conv_out"], gate_c_src, tm_n, d)
    mix_in = _matmul(o, [(p["w_ret_out"], 0)],
                     [(gate_r_src[0], _tile_spec(tm, tn, gate_r_src[1])), (y_c, _tile_spec(tm, tn, 0))],
                     _epi_gate_mul_add, d, BF16, tm, tn, "ret_out_proj")

    x1, h2 = _matmul_norm_residual(mix_in, p["w_out"], x2, p["norm_mix_post"], p["norm_ffn_pre"],
                                   tm_n, d, "out_proj_norm")

    d_ff = p["w_ffn_gate"].shape[1]
    f = _matmul(h2, [(p["w_ffn_gate"], 0), (p["w_ffn_up"], 0)], [], _epi_swiglu,
                d_ff, BF16, tm_big, 512, "ffn_up")
    y = _matmul_norm_residual(f, p["w_ffn_down"], x1, p["norm_ffn_post"], None, min(rows, 256), d_ff, "ffn_down_norm")
    return y.reshape(bsz, seq, d), new_hist, s_new


def kernel(x_prompt, x_sample, cache_conv, state_ret, norm_mix_pre, norm_mix_post, w_in, b_gate, conv_dw, conv_dw_b, conv_ln_g, conv_ln_b, w_conv_out, ret_gn_g, w_ret_out, w_out, norm_ffn_pre, norm_ffn_post, w_ffn_gate, w_ffn_up, w_ffn_down):
    depth = w_in.shape[0]
    past_len = 1024
    y_prompt, y_sample = x_prompt, x_sample
    conv_p, ret_p, conv_s, ret_s = [], [], [], []
    n_batch = x_prompt.shape[0]
    for l in range(depth):
        p = {
            "norm_mix_pre": norm_mix_pre[l], "norm_mix_post": norm_mix_post[l],
            "w_in": w_in[l].astype(BF16), "b_gate": b_gate[l],
            "conv_dw": conv_dw[l], "conv_dw_b": conv_dw_b[l],
            "conv_ln_g": conv_ln_g[l], "conv_ln_b": conv_ln_b[l],
            "w_conv_out": w_conv_out[l].astype(BF16), "ret_gn_g": ret_gn_g[l],
            "w_ret_out": w_ret_out[l].astype(BF16), "w_out": w_out[l].astype(BF16),
            "norm_ffn_pre": norm_ffn_pre[l], "norm_ffn_post": norm_ffn_post[l],
            "w_ffn_gate": w_ffn_gate[l].astype(BF16), "w_ffn_up": w_ffn_up[l].astype(BF16),
            "w_ffn_down": w_ffn_down[l].astype(BF16),
        }
        zero_hist = jnp.zeros((n_batch, CONV_WIDTH - 1, x_prompt.shape[-1]), x_prompt.dtype)
        zero_state = jnp.zeros((n_batch, N_RET_HEADS, RET_DK, RET_DV), x_prompt.dtype)
        y_prompt, hp, sp = _trunk_layer(y_prompt, zero_hist, zero_state, 0, p)
        y_sample, hs, ss = _trunk_layer(y_sample, cache_conv[l], state_ret[l], past_len, p)
        conv_p.append(hp)
        ret_p.append(sp)
        conv_s.append(hs)
        ret_s.append(ss)
    return (y_prompt, y_sample, jnp.stack(conv_p), jnp.stack(ret_p), jnp.stack(conv_s), jnp.stack(ret_s))
```
